```python
import math
import jax
import jax.numpy as jnp
from jax import lax
import numpy as np

D_MODEL = 1024
BATCH = 4
SEQ = 8192
DEPTH = 4
DEC_BATCH = 2
DEC_SEQ = 8192
PAST_LEN = 128

CHUNK = 64
CONV_K = 5
NORM_EPS = 1e-6
D_FF = 2816
N_BRANCH = 3
MIX_W = 512

GDN_HEADS = 4
GDN_DK = 128
GDN_DV = 128
GDN_KW = GDN_HEADS * GDN_DK
GDN_VW = GDN_HEADS * GDN_DV
GDN_CONV_CH = 2 * GDN_KW + GDN_VW

GLA_HEADS = 4
GLA_DK = 64
GLA_DV = 128
GLA_KW = GLA_HEADS * GLA_DK
GLA_VW = GLA_HEADS * GLA_DV
GLA_RANK = 16
GLA_TAU = 16.0

SSD_HEADS = 8
SSD_HEADDIM = 64
SSD_GROUPS = 2
SSD_STATE = 128
SSD_HPG = SSD_HEADS // SSD_GROUPS
SSD_INNER = SSD_HEADS * SSD_HEADDIM
SSD_BCW = SSD_GROUPS * SSD_STATE
SSD_CONV_CH = SSD_INNER + 2 * SSD_BCW

IN_SIZES = (GDN_CONV_CH, GDN_VW, 2 * GDN_HEADS, 2 * GDN_HEADS,
            GLA_KW, GLA_KW, GLA_VW, GLA_VW, 2 * GLA_RANK,
            SSD_INNER, SSD_CONV_CH, 2 * SSD_HEADS,
            N_BRANCH * D_MODEL)
IN_COLS = sum(IN_SIZES)

kernel_name = "hybrid_bidir_gdn_gla_ssd_macaron"

F32 = jnp.float32


def _rmsnorm(x, g):
    xf = x.astype(F32)
    y = xf * lax.rsqrt(jnp.mean(xf * xf, axis=-1, keepdims=True) + NORM_EPS)
    return (y * g.astype(F32)).astype(x.dtype)


def _l2norm(t):
    return t * lax.rsqrt(jnp.sum(t * t, axis=-1, keepdims=True) + NORM_EPS)


def _swiglu(h, w_gate, w_up, w_down):
    return (jax.nn.silu(h @ w_gate) * (h @ w_up)) @ w_down


def _split_cols(z, sizes):
    parts, start = [], 0
    for s in sizes:
        parts.append(z[..., start:start + s])
        start += s
    return parts


def _dwconv_centred(u, w, b):
    ch = u.shape[-1]
    y = lax.conv_general_dilated(
        u, w.astype(u.dtype)[:, None, :], window_strides=(1,),
        padding=((CONV_K // 2, CONV_K // 2),),
        dimension_numbers=('NWC', 'WIO', 'NWC'), feature_group_count=ch)
    if b is not None:
        y = y + b.astype(u.dtype)
    return y


def _flip(t):
    return jnp.flip(t, axis=1)


def _to_chunks(t):
    b, l = t.shape[:2]
    return jnp.moveaxis(t.reshape((b, l // CHUNK, CHUNK) + t.shape[2:]), 1, 0)


def _from_chunks(t):
    t = jnp.moveaxis(t, 0, 1)
    return t.reshape((t.shape[0], t.shape[1] * t.shape[2]) + t.shape[3:])


def _gated_delta_chunked(q, k, v, g, beta):
    bsz, _, h, dk = q.shape
    dv = v.shape[-1]
    qc, kc, vc, gc, bc = (_to_chunks(t) for t in (q, k, v, g, beta))
    incl = jnp.tril(jnp.ones((CHUNK, CHUNK), bool))
    strict = jnp.tril(jnp.ones((CHUNK, CHUNK), bool), -1)
    G = jnp.cumsum(gc, axis=2)
    Gh = jnp.moveaxis(G, 2, -1)
    diff = Gh[..., :, None] - Gh[..., None, :]
    kb = kc * bc[..., None]
    a_mat = jnp.einsum('nbihd,nbjhd->nbhij', kb, kc) * jnp.exp(jnp.where(strict, diff, -jnp.inf))
    rhs = jnp.concatenate([kb * jnp.exp(G)[..., None], vc * bc[..., None]], axis=-1)
    rhs = jnp.swapaxes(rhs, 2, 3)
    eye = jnp.eye(CHUNK, dtype=F32)
    sol = lax.linalg.triangular_solve(eye + a_mat, rhs, left_side=True, lower=True,
                                      unit_diagonal=True)
    w_c, u_c = sol[..., :dk], sol[..., dk:]
    att = jnp.einsum('nbihd,nbjhd->nbhij', qc, kc) * jnp.exp(jnp.where(incl, diff, -jnp.inf))
    qg = jnp.swapaxes(qc * jnp.exp(G)[..., None], 2, 3)
    kg = jnp.swapaxes(kc * jnp.exp(G[:, :, -1:] - G)[..., None], 2, 3)
    gl = jnp.exp(G[:, :, -1])

    def step(S, xs):
        w_, u_, att_, qg_, kg_, gl_ = xs
        v_new = u_ - jnp.einsum('bhcd,bhde->bhce', w_, S)
        o = jnp.einsum('bhcd,bhde->bhce', qg_, S) + jnp.einsum('bhij,bhje->bhie', att_, v_new)
        S = S * gl_[..., None, None] + jnp.einsum('bhcd,bhce->bhde', kg_, v_new)
        return S, o

    _, o = lax.scan(step, jnp.zeros((bsz, h, dk, dv), F32), (w_c, u_c, att, qg, kg, gl))
    return _from_chunks(jnp.swapaxes(o, 2, 3))


def _gla_chunked(q, k, v, gk):
    bsz, _, h, dk = q.shape
    dv = v.shape[-1]
    qc, kc, vc, gc = (_to_chunks(t) for t in (q, k, v, gk))
    Gc = jnp.cumsum(gc, axis=2)
    incl = jnp.tril(jnp.ones((CHUNK, CHUNK), bool))[None, :, :, None, None]

    def step(S, xs):
        q_, k_, v_, G_ = xs
        Gl = G_[:, -1]
        o = jnp.einsum('bchd,bhde->bche', q_ * jnp.exp(G_), S)
        diff = G_[:, :, None] - G_[:, None, :]
        dec = jnp.exp(jnp.where(incl, diff, -jnp.inf))
        att = jnp.einsum('bihd,bjhd,bijhd->bhij', q_, k_, dec)
        o = o + jnp.einsum('bhij,bjhe->bihe', att, v_)
        S = S * jnp.exp(Gl)[..., None] + jnp.einsum('bchd,bche->bhde', k_ * jnp.exp(Gl[:, None] - G_), v_)
        return S, o

    _, o = lax.scan(step, jnp.zeros((bsz, h, dk, dv), F32), (qc, kc, vc, Gc))
    return _from_chunks(o)


def _ssd_chunked(x, dt, A, Bm, Cm):
    bsz = x.shape[0]
    xc, dtc, Bc, Cc = (_to_chunks(t) for t in (x, dt, Bm, Cm))
    incl = jnp.tril(jnp.ones((CHUNK, CHUNK), bool))[:, :, None, None]
    acum = jnp.cumsum(dtc * A, axis=2)
    diff = acum[:, :, :, None] - acum[:, :, None, :]
    lmat = jnp.exp(jnp.where(incl, diff, -jnp.inf))
    scores = jnp.einsum('nbigs,nbjgs->nbijg', Cc, Bc)[..., None] * lmat * dtc[:, :, None]
    y_diag = jnp.einsum('nbijgh,nbjghp->nbighp', scores, xc)
    w_state = jnp.exp(acum[:, :, -1:] - acum) * dtc

    def step(S, xs):
        c_, b_, ac_, w_, x_ = xs
        y_off = jnp.einsum('bcgs,bghps->bcghp', c_, S) * jnp.exp(ac_)[..., None]
        S = S * jnp.exp(ac_[:, -1])[..., None, None] + jnp.einsum('bcgs,bcgh,bcghp->bghps', b_, w_, x_)
        return S, y_off

    S0 = jnp.zeros((bsz, SSD_GROUPS, SSD_HPG, SSD_HEADDIM, SSD_STATE), F32)
    _, y_off = lax.scan(step, S0, (Cc, Bc, acum, w_state, xc))
    return _from_chunks(y_diag + y_off)


def _gdn_mixer(qkv_raw, z, b_raw, a_raw, conv_w, a_log, dt_bias, norm_g):
    bsz, l = z.shape[:2]
    qkv = jax.nn.silu(_dwconv_centred(qkv_raw, conv_w, None)).astype(F32)
    q = _l2norm(qkv[..., :GDN_KW].reshape(bsz, l, GDN_HEADS, GDN_DK)) * (GDN_DK ** -0.5)
    k = _l2norm(qkv[..., GDN_KW:2 * GDN_KW].reshape(bsz, l, GDN_HEADS, GDN_DK))
    v = qkv[..., 2 * GDN_KW:].reshape(bsz, l, GDN_HEADS, GDN_DV)
    beta = jax.nn.sigmoid(b_raw.astype(F32)).reshape(bsz, l, 2, GDN_HEADS)
    g = -jnp.exp(a_log.astype(F32)) * jax.nn.softplus(
        a_raw.astype(F32).reshape(bsz, l, 2, GDN_HEADS) + dt_bias.astype(F32))
    o = (_gated_delta_chunked(q, k, v, g[:, :, 0], beta[:, :, 0])
         + _flip(_gated_delta_chunked(_flip(q), _flip(k), _flip(v),
                                      _flip(g[:, :, 1]), _flip(beta[:, :, 1]))))
    o = _rmsnorm(o, norm_g) * jax.nn.silu(z.astype(F32).reshape(bsz, l, GDN_HEADS, GDN_DV))
    return o.reshape(bsz, l, GDN_VW).astype(z.dtype)


def _gla_mixer(q_raw, k_raw, v_raw, r, g_lr, w_gup, b_g, norm_g):
    bsz, l = r.shape[:2]
    q = q_raw.astype(F32).reshape(bsz, l, GLA_HEADS, GLA_DK) * (GLA_DK ** -0.5)
    k = k_raw.astype(F32).reshape(bsz, l, GLA_HEADS, GLA_DK)
    v = v_raw.astype(F32).reshape(bsz, l, GLA_HEADS, GLA_DV)
    lr = g_lr.astype(F32).reshape(bsz, l, 2, GLA_RANK)
    gk = jax.nn.log_sigmoid(jnp.einsum('blsr,srk->blsk', lr, w_gup.astype(F32))
                            + b_g.astype(F32)) / GLA_TAU
    gk = gk.reshape(bsz, l, 2, GLA_HEADS, GLA_DK)
    o = (_gla_chunked(q, k, v, gk[:, :, 0])
         + _flip(_gla_chunked(_flip(q), _flip(k), _flip(v), _flip(gk[:, :, 1]))))
    o = _rmsnorm(o, norm_g) * jax.nn.silu(r.astype(F32).reshape(bsz, l, GLA_HEADS, GLA_DV))
    return o.reshape(bsz, l, GLA_VW).astype(r.dtype)


def _ssd_mixer(z, xbc_raw, dt_raw, conv_w, conv_b, a_log, dt_bias, d_skip, norm_g):
    bsz, l = z.shape[:2]
    xbc = jax.nn.silu(_dwconv_centred(xbc_raw, conv_w, conv_b)).astype(F32)
    x = xbc[..., :SSD_INNER].reshape(bsz, l, SSD_GROUPS, SSD_HPG, SSD_HEADDIM)
    Bm = xbc[..., SSD_INNER:SSD_INNER + SSD_BCW].reshape(bsz, l, SSD_GROUPS, SSD_STATE)
    Cm = xbc[..., SSD_INNER + SSD_BCW:].reshape(bsz, l, SSD_GROUPS, SSD_STATE)
    dt = jax.nn.softplus(dt_raw.astype(F32).reshape(bsz, l, 2, SSD_HEADS) + dt_bias.astype(F32))
    dt = dt.reshape(bsz, l, 2, SSD_GROUPS, SSD_HPG)
    A = -jnp.exp(a_log.astype(F32)).reshape(2, SSD_GROUPS, SSD_HPG)
    y = (_ssd_chunked(x, dt[:, :, 0], A[0], Bm, Cm)
         + _flip(_ssd_chunked(_flip(x), _flip(dt[:, :, 1]), A[1], _flip(Bm), _flip(Cm))))
    y = y + x * d_skip.astype(F32).reshape(SSD_GROUPS, SSD_HPG)[..., None]
    y = y.reshape(bsz, l, SSD_INNER) * jax.nn.silu(z.astype(F32))
    y = _rmsnorm(y.reshape(bsz, l, SSD_GROUPS, SSD_INNER // SSD_GROUPS),
                 norm_g.reshape(SSD_GROUPS, SSD_INNER // SSD_GROUPS))
    return y.reshape(bsz, l, SSD_INNER).astype(z.dtype)


def _layer(x, ffn_norm, ffn_w_gate, ffn_w_up, ffn_w_down, mix_norm, w_in,
           gdn_conv_w, gdn_a_log, gdn_dt_bias, gdn_norm, gla_w_gup, gla_b_g, gla_norm,
           ssd_conv_w, ssd_conv_b, ssd_a_log, ssd_dt_bias, ssd_d, ssd_norm, w_branch, w_out):
    bsz, l = x.shape[:2]
    x = x + 0.5 * _swiglu(_rmsnorm(x, ffn_norm[0]), ffn_w_gate[0], ffn_w_up[0], ffn_w_down[0])
    h = _rmsnorm(x, mix_norm)
    zin = h @ w_in
    (a_qkv, a_z, a_b, a_a, b_q, b_k, b_v, b_r, b_g, c_z, c_xbc, c_dt,
     gate_raw) = _split_cols(zin, IN_SIZES)
    y_a = _gdn_mixer(a_qkv, a_z, a_b, a_a, gdn_conv_w, gdn_a_log, gdn_dt_bias, gdn_norm)
    y_b = _gla_mixer(b_q, b_k, b_v, b_r, b_g, gla_w_gup, gla_b_g, gla_norm)
    y_c = _ssd_mixer(c_z, c_xbc, c_dt, ssd_conv_w, ssd_conv_b, ssd_a_log, ssd_dt_bias, ssd_d, ssd_norm)
    gates = jax.nn.sigmoid(gate_raw).reshape(bsz, l, N_BRANCH, D_MODEL)
    merged = (gates[:, :, 0] * (y_a @ w_branch[0])
              + gates[:, :, 1] * (y_b @ w_branch[1])
              + gates[:, :, 2] * (y_c @ w_branch[2]))
    x = x + merged @ w_out
    x = x + 0.5 * _swiglu(_rmsnorm(x, ffn_norm[1]), ffn_w_gate[1], ffn_w_up[1], ffn_w_down[1])
    return x


def setup_inputs(seed: int = 0) -> dict:
    key = jax.random.key(seed)
    ks = jax.random.split(key, 24)

    def nrm(k, shape, fan_in):
        return jax.random.normal(k, shape, F32) * (fan_in ** -0.5)

    def gain(k, shape):
        return 1.0 + 0.02 * jax.random.normal(k, shape, F32)

    def a_log_init(k, shape):
        return jnp.log(jax.random.uniform(k, shape, F32, minval=1.0, maxval=16.0))

    def dt_bias_init(k, shape):
        dt = jnp.exp(jax.random.uniform(k, shape, F32, minval=math.log(1e-3), maxval=math.log(1e-1)))
        return dt + jnp.log(-jnp.expm1(-dt))

    return {
        "x_prompt": jax.random.normal(ks[0], (BATCH, SEQ, D_MODEL), F32),
        "x_sample": jax.random.normal(ks[1], (DEC_BATCH, DEC_SEQ, D_MODEL), F32),
        "ffn_norm": gain(ks[2], (DEPTH, 2, D_MODEL)),
        "ffn_w_gate": nrm(ks[3], (DEPTH, 2, D_MODEL, D_FF), D_MODEL),
        "ffn_w_up": nrm(ks[4], (DEPTH, 2, D_MODEL, D_FF), D_MODEL),
        "ffn_w_down": nrm(ks[5], (DEPTH, 2, D_FF, D_MODEL), D_FF),
        "mix_norm": gain(ks[6], (DEPTH, D_MODEL)),
        "w_in": nrm(ks[7], (DEPTH, D_MODEL, IN_COLS), D_MODEL),
        "gdn_conv_w": nrm(ks[8], (DEPTH, CONV_K, GDN_CONV_CH), CONV_K),
        "gdn_a_log": a_log_init(ks[9], (DEPTH, 2, GDN_HEADS)),
        "gdn_dt_bias": dt_bias_init(ks[10], (DEPTH, 2, GDN_HEADS)),
        "gdn_norm": gain(ks[11], (DEPTH, GDN_DV)),
        "gla_w_gup": nrm(ks[12], (DEPTH, 2, GLA_RANK, GLA_KW), GLA_RANK),
        "gla_b_g": 0.1 * jax.random.normal(ks[13], (DEPTH, 2, GLA_KW), F32),
        "gla_norm": gain(ks[14], (DEPTH, GLA_DV)),
        "ssd_conv_w": nrm(ks[15], (DEPTH, CONV_K, SSD_CONV_CH), CONV_K),
        "ssd_conv_b": 0.02 * jax.random.normal(ks[16], (DEPTH, SSD_CONV_CH), F32),
        "ssd_a_log": a_log_init(ks[17], (DEPTH, 2, SSD_HEADS)),
        "ssd_dt_bias": dt_bias_init(ks[18], (DEPTH, 2, SSD_HEADS)),
        "ssd_d": gain(ks[19], (DEPTH, SSD_HEADS)),
        "ssd_norm": gain(ks[20], (DEPTH, SSD_INNER)),
        "w_branch": nrm(ks[21], (DEPTH, N_BRANCH, MIX_W, D_MODEL), MIX_W),
        "w_out": nrm(ks[22], (DEPTH, D_MODEL, D_MODEL), D_MODEL),
        "final_norm": gain(ks[23], (D_MODEL,)),
    }


def reference(x_prompt, x_sample, ffn_norm, ffn_w_gate, ffn_w_up, ffn_w_down, mix_norm, w_in,
              gdn_conv_w, gdn_a_log, gdn_dt_bias, gdn_norm, gla_w_gup, gla_b_g, gla_norm,
              ssd_conv_w, ssd_conv_b, ssd_a_log, ssd_dt_bias, ssd_d, ssd_norm,
              w_branch, w_out, final_norm):
    def trunk(x):
        for i in range(DEPTH):
            x = _layer(x, ffn_norm[i], ffn_w_gate[i], ffn_w_up[i], ffn_w_down[i], mix_norm[i], w_in[i],
                       gdn_conv_w[i], gdn_a_log[i], gdn_dt_bias[i], gdn_norm[i],
                       gla_w_gup[i], gla_b_g[i], gla_norm[i],
                       ssd_conv_w[i], ssd_conv_b[i], ssd_a_log[i], ssd_dt_bias[i], ssd_d[i], ssd_norm[i],
                       w_branch[i], w_out[i])
        return _rmsnorm(x, final_norm)

    y_prompt = trunk(x_prompt)
    y_sample = trunk(x_sample)
    return (y_prompt, y_sample)
```

```python
import functools

import numpy as np
import jax
import jax.numpy as jnp
from jax import lax
from jax.experimental import pallas as pl
from jax.experimental.pallas import tpu as pltpu

F32 = jnp.float32
BF16 = jnp.bfloat16

D_MODEL = 1024
DEPTH = 4
CHUNK = 64
CONV_K = 5
NORM_EPS = 1e-6
D_FF = 2816
N_BRANCH = 3
MIX_W = 512

GDN_HEADS = 4
GDN_DK = 128
GDN_DV = 128
GDN_KW = GDN_HEADS * GDN_DK
GDN_VW = GDN_HEADS * GDN_DV
GDN_CONV_CH = 2 * GDN_KW + GDN_VW

GLA_HEADS = 4
GLA_DK = 64
GLA_DV = 128
GLA_KW = GLA_HEADS * GLA_DK
GLA_VW = GLA_HEADS * GLA_DV
GLA_RANK = 16
GLA_TAU = 16.0

SSD_HEADS = 8
SSD_HEADDIM = 64
SSD_GROUPS = 2
SSD_STATE = 128
SSD_HPG = SSD_HEADS // SSD_GROUPS
SSD_INNER = SSD_HEADS * SSD_HEADDIM
SSD_BCW = SSD_GROUPS * SSD_STATE
SSD_CONV_CH = SSD_INNER + 2 * SSD_BCW

IN_SIZES = (GDN_CONV_CH, GDN_VW, 2 * GDN_HEADS, 2 * GDN_HEADS,
            GLA_KW, GLA_KW, GLA_VW, GLA_VW, 2 * GLA_RANK,
            SSD_INNER, SSD_CONV_CH, 2 * SSD_HEADS,
            N_BRANCH * D_MODEL)
IN_COLS = sum(IN_SIZES)
IN_OFFS = tuple(int(v) for v in np.cumsum((0,) + IN_SIZES[:-1]))

LANES = 128
SUBLANES = 8
VMEM_LIMIT = 56 * 1024 * 1024

HALO = SUBLANES
TM_DENSE = 512
TM_MERGE = 256
FF_TILE = 256
MIX_ROWS = 256
NEG_BIG = -1e30

GDN_SLAB = GDN_CONV_CH + GDN_VW + 2 * LANES
GLA_SLAB = 2 * GLA_KW + 2 * GLA_VW + LANES
SSD_SLAB = SSD_INNER + SSD_CONV_CH + 2 * LANES
GATE_SLAB = N_BRANCH * D_MODEL


def _silu(x):
    return x * jax.nn.sigmoid(x)


def _softplus(x):
    return jnp.maximum(x, 0.0) + jnp.log1p(jnp.exp(-jnp.abs(x)))


def _rms(x, g):
    ms = jnp.mean(x * x, axis=-1, keepdims=True)
    return x * lax.rsqrt(ms + NORM_EPS) * g


def _dot(a, b):
    return jnp.dot(a, b, preferred_element_type=F32)


def _dot_nt(a, b):
    return lax.dot_general(a, b, (((1,), (1,)), ((), ())), preferred_element_type=F32)


def _dot_tn(a, b):
    return lax.dot_general(a, b, (((0,), (0,)), ((), ())), preferred_element_type=F32)


def _bf(x):
    return x.astype(BF16)


def _split_terms(x, n):
    terms = []
    r = x
    for t in range(n):
        p = r.astype(BF16)
        terms.append(p)
        if t + 1 < n:
            r = r - p.astype(F32)
    return terms


def _mask_dot(m01, x, n=3):
    acc = None
    for p in _split_terms(x, n):
        t = _dot(m01, p)
        acc = t if acc is None else acc + t
    return acc


def _dot_mask(x, m01, n=3):
    acc = None
    for p in _split_terms(x, n):
        t = _dot(p, m01)
        acc = t if acc is None else acc + t
    return acc


def _dot3(a, b):
    ah, al = _split_terms(a, 2)
    bh, bl = _split_terms(b, 2)
    return _dot(ah, bh) + (_dot(ah, bl) + _dot(al, bh))


def _scan_masks(direction):
    ri = lax.broadcasted_iota(jnp.int32, (CHUNK, CHUNK), 0)
    ci = lax.broadcasted_iota(jnp.int32, (CHUNK, CHUNK), 1)
    if direction == 0:
        return ci <= ri, ci < ri
    return ci >= ri, ci > ri


def _unit_tri_inverse(a):
    ri = lax.broadcasted_iota(jnp.int32, (CHUNK, CHUNK), 0)
    ci = lax.broadcasted_iota(jnp.int32, (CHUNK, CHUNK), 1)
    eye = jnp.where(ri == ci, 1.0, 0.0).astype(F32)
    s1 = _dot3(a, a)
    s2 = _dot3(s1, s1)
    s3 = _dot3(s2, s2)
    s4 = _dot3(s3, s3)
    s5 = _dot3(s4, s4)
    p1 = _dot3(eye - a, eye + s1)
    p2 = _dot3(eye + s2, eye + s3)
    p3 = _dot3(eye + s4, eye + s5)
    return _dot3(_dot3(p1, p2), p3)


def _conv_window(win, w_ref, cols, bias=None):
    acc = None
    for j in range(CONV_K):
        lo = HALO - CONV_K // 2 + j
        t = w_ref[j:j + 1, cols] * win[lo:lo + CHUNK]
        acc = t if acc is None else acc + t
    if bias is not None:
        acc = acc + bias
    return acc


def _stage_halo(xp_ref, d, main, prev, nxt, blk, nblk, rows):
    xp_ref[d, 0:HALO, :] = jnp.where(blk == 0, 0.0, prev[0])
    xp_ref[d, HALO:HALO + rows, :] = main[0]
    xp_ref[d, HALO + rows:rows + 2 * HALO, :] = jnp.where(blk == nblk - 1, 0.0, nxt[0])


def _ffn_kernel(x_ref, g_ref, wg_ref, wu_ref, wd_ref, *rest, final_norm):
    if final_norm:
        fg_ref, o_ref, acc_ref = rest
    else:
        o_ref, acc_ref = rest
    x = x_ref[...]
    h = _bf(_rms(x, g_ref[...]))
    for c in range(D_FF // FF_TILE):
        sl = slice(c * FF_TILE, (c + 1) * FF_TILE)
        gate = _dot(h, wg_ref[:, sl])
        up = _dot(h, wu_ref[:, sl])
        act = _bf(_silu(gate) * up)
        part = _dot(act, wd_ref[sl, :])
        if c == 0:
            acc_ref[...] = part
        else:
            acc_ref[...] += part
    y = x + 0.5 * acc_ref[...]
    if final_norm:
        y = _rms(y, fg_ref[...])
    o_ref[...] = y


def _resident(shape):
    return pl.BlockSpec(shape, lambda *_: (0,) * len(shape), pipeline_mode=pl.Buffered(1))


def _ffn(x2d, norm_g, wg, wu, wd, final_g=None):
    t = x2d.shape[0]
    tm = min(TM_DENSE, t)
    row = pl.BlockSpec((tm, D_MODEL), lambda i: (i, 0))
    in_specs = [row, _resident((1, D_MODEL)), _resident((D_MODEL, D_FF)),
                _resident((D_MODEL, D_FF)), _resident((D_FF, D_MODEL))]
    args = [x2d, norm_g.reshape(1, D_MODEL), wg, wu, wd]
    if final_g is not None:
        in_specs.append(_resident((1, D_MODEL)))
        args.append(final_g.reshape(1, D_MODEL))
    return pl.pallas_call(
        functools.partial(_ffn_kernel, final_norm=final_g is not None),
        out_shape=jax.ShapeDtypeStruct((t, D_MODEL), F32),
        grid=(t // tm,),
        in_specs=in_specs,
        out_specs=row,
        scratch_shapes=[pltpu.VMEM((tm, D_MODEL), F32)],
        compiler_params=pltpu.CompilerParams(
            dimension_semantics=("arbitrary",), vmem_limit_bytes=VMEM_LIMIT),
        name="ffn_final" if final_g is not None else "ffn",
    )(*args)


def _inproj_kernel(x_ref, g_ref, w_ref, o_ref):
    h = _bf(_rms(x_ref[...], g_ref[...]))
    o_ref[...] = _dot(h, w_ref[...])


def _inproj(x2d, norm_g, w):
    t = x2d.shape[0]
    n = w.shape[1]
    tm = min(TM_DENSE, t)
    return pl.pallas_call(
        _inproj_kernel,
        out_shape=jax.ShapeDtypeStruct((t, n), F32),
        grid=(t // tm,),
        in_specs=[pl.BlockSpec((tm, D_MODEL), lambda i: (i, 0)),
                  _resident((1, D_MODEL)), _resident((D_MODEL, n))],
        out_specs=pl.BlockSpec((tm, n), lambda i: (i, 0)),
        compiler_params=pltpu.CompilerParams(
            dimension_semantics=("arbitrary",), vmem_limit_bytes=VMEM_LIMIT),
        name="inproj",
    )(x2d, norm_g.reshape(1, D_MODEL), w)


def _head_rms(o, g_row, width):
    parts = []
    for s in range(0, o.shape[-1], width):
        blk = o[:, s:s + width]
        ms = jnp.mean(blk * blk, axis=-1, keepdims=True)
        parts.append(blk * lax.rsqrt(ms + NORM_EPS))
    return jnp.concatenate(parts, axis=-1) * g_row


def _merge_kernel(x_ref, oaf_ref, oab_ref, za_ref, obf_ref, obb_ref, rb_ref,
                  ocf_ref, ocb_ref, zc_ref, gate_ref, na_ref, nb_ref, nc_ref,
                  wbr_ref, wout_ref, o_ref):
    ya = _head_rms(oaf_ref[...] + oab_ref[...], na_ref[...], GDN_DV) * _silu(za_ref[...])
    yb = _head_rms(obf_ref[...] + obb_ref[...], nb_ref[...], GLA_DV) * _silu(rb_ref[...])
    yc = (ocf_ref[...] + ocb_ref[...]) * _silu(zc_ref[...])
    yc = _head_rms(yc, nc_ref[...], SSD_INNER // SSD_GROUPS)
    merged = None
    for b, y in enumerate((ya, yb, yc)):
        gate = jax.nn.sigmoid(gate_ref[:, b * D_MODEL:(b + 1) * D_MODEL])
        t = gate * _dot(_bf(y), wbr_ref[b])
        merged = t if merged is None else merged + t
    o_ref[...] = x_ref[...] + _dot(_bf(merged), wout_ref[...])


def _merge(x2d, oa, za_slab, ob, zb_slab, oc, zc_slab, gates, na, nb, nc, wbr, wout):
    t = x2d.shape[0]
    tm = min(TM_MERGE, t)

    def col(width, blk):
        return pl.BlockSpec((tm, width), lambda i: (i, blk))

    in_specs = [
        col(D_MODEL, 0),
        col(MIX_W, 0), col(MIX_W, 0), col(MIX_W, GDN_CONV_CH // MIX_W),
        col(MIX_W, 0), col(MIX_W, 0), col(MIX_W, (2 * GLA_KW + GLA_VW) // MIX_W),
        col(MIX_W, 0), col(MIX_W, 0), col(MIX_W, SSD_CONV_CH // MIX_W),
        col(GATE_SLAB, 0),
        _resident((1, MIX_W)), _resident((1, MIX_W)), _resident((1, MIX_W)),
        _resident((N_BRANCH, MIX_W, D_MODEL)), _resident((D_MODEL, D_MODEL)),
    ]
    return pl.pallas_call(
        _merge_kernel,
        out_shape=jax.ShapeDtypeStruct((t, D_MODEL), F32),
        grid=(t // tm,),
        in_specs=in_specs,
        out_specs=col(D_MODEL, 0),
        compiler_params=pltpu.CompilerParams(
            dimension_semantics=("arbitrary",), vmem_limit_bytes=VMEM_LIMIT),
        name="merge",
    )(x2d, oa[0], oa[1], za_slab, ob[0], ob[1], zb_slab, oc[0], oc[1], zc_slab, gates,
      na, nb, nc, wbr, wout)


def _gdn_prep_chunk(d, c, xp_ref, small, convw_ref, prow_ref,
                    wq_ref, u_ref, att_ref, kg_ref, gl_ref):
    incl, strict = _scan_masks(d)
    m_incl = jnp.where(incl, 1.0, 0.0).astype(BF16)
    base = pl.multiple_of(c * CHUNK, CHUNK)
    sm = small[0, pl.ds(base, CHUNK), :]
    neg_a = prow_ref[2 * d:2 * d + 1, :]
    dtb = prow_ref[2 * d + 1:2 * d + 2, :]
    beta_all = jax.nn.sigmoid(sm)
    g_all = neg_a * _softplus(sm + dtb)
    gcum = _mask_dot(m_incl, g_all)
    gcum_t = gcum.T
    last = CHUNK - 1 if d == 0 else 0
    for h in range(GDN_HEADS):
        def conv(off):
            cols = slice(off + h * LANES, off + (h + 1) * LANES)
            win = xp_ref[d, pl.ds(base, CHUNK + 2 * HALO), cols]
            return _silu(_conv_window(win, convw_ref, cols))
        q = conv(0)
        k = conv(GDN_KW)
        v = conv(2 * GDN_KW)
        q = q * lax.rsqrt(jnp.sum(q * q, axis=-1, keepdims=True) + NORM_EPS) * (GDN_DK ** -0.5)
        k = k * lax.rsqrt(jnp.sum(k * k, axis=-1, keepdims=True) + NORM_EPS)
        beta = beta_all[:, h:h + 1]
        g_col = gcum[:, GDN_HEADS + h:GDN_HEADS + h + 1]
        g_row = gcum_t[GDN_HEADS + h:GDN_HEADS + h + 1, :]
        diff = g_col - g_row
        dec_strict = jnp.exp(jnp.where(strict, diff, NEG_BIG))
        dec_incl = jnp.exp(jnp.where(incl, diff, NEG_BIG))
        kb = k * beta
        kbf = _bf(k)
        a_mat = _dot_nt(_bf(kb), kbf) * dec_strict
        t_inv = _unit_tri_inverse(a_mat)
        e_g = jnp.exp(g_col)
        w = _dot3(t_inv, kb * e_g)
        u = _dot3(t_inv, v * beta)
        att = _dot_nt(_bf(q), kbf) * dec_incl
        g_last = g_col[last:last + 1, :]
        wq_ref[d, c, h, 0:CHUNK, :] = _bf(w)
        wq_ref[d, c, h, CHUNK:2 * CHUNK, :] = _bf(q * e_g)
        u_ref[d, c, h] = u
        att_ref[d, c, h] = _bf(att)
        kg_ref[d, c, h] = _bf(k * jnp.exp(g_last - g_col))
        gl_ref[d, c, h] = jnp.broadcast_to(jnp.exp(g_last), (SUBLANES, LANES))


def _gdn_scan_chunk(d, c, wq_ref, u_ref, att_ref, kg_ref, gl_ref, s_ref, o_ref):
    base = pl.multiple_of(c * CHUNK, CHUNK)
    for h in range(GDN_HEADS):
        s = s_ref[d, h]
        r = _dot(wq_ref[d, c, h], _bf(s))
        v_new = u_ref[d, c, h] - r[0:CHUNK]
        vb = _bf(v_new)
        o = r[CHUNK:2 * CHUNK] + _dot(att_ref[d, c, h], vb)
        s_ref[d, h] = s * gl_ref[d, c, h, 0:1, :] + _dot_tn(kg_ref[d, c, h], vb)
        o_ref[0, pl.ds(base, CHUNK), h * LANES:(h + 1) * LANES] = o


def _gdn_kernel(main_f, prev_f, next_f, small_f, main_b, prev_b, next_b, small_b,
                convw_ref, prow_ref, of_ref, ob_ref,
                xp_ref, wq_ref, u_ref, att_ref, kg_ref, gl_ref, s_ref, *, rows, nblk):
    i = pl.program_id(1)
    nc = rows // CHUNK

    @pl.when(i == 0)
    def _():
        s_ref[...] = jnp.zeros_like(s_ref)

    _stage_halo(xp_ref, 0, main_f, prev_f, next_f, i, nblk, rows)
    _stage_halo(xp_ref, 1, main_b, prev_b, next_b, nblk - 1 - i, nblk, rows)

    def prep(c, carry):
        for d, small in ((0, small_f), (1, small_b)):
            _gdn_prep_chunk(d, c, xp_ref, small, convw_ref, prow_ref,
                            wq_ref, u_ref, att_ref, kg_ref, gl_ref)
        return carry

    lax.fori_loop(0, nc, prep, 0)

    def scan(j, carry):
        _gdn_scan_chunk(0, j, wq_ref, u_ref, att_ref, kg_ref, gl_ref, s_ref, of_ref)
        _gdn_scan_chunk(1, nc - 1 - j, wq_ref, u_ref, att_ref, kg_ref, gl_ref, s_ref, ob_ref)
        return carry

    lax.fori_loop(0, nc, scan, 0)


def _bidir_specs(rows, nblk, width, col_blk, halo):
    rb = rows // HALO
    last = nblk * rb - 1

    def mk(blk_of):
        specs = [pl.BlockSpec((1, rows, width), lambda b, i: (b, blk_of(i), col_blk))]
        if halo:
            specs.append(pl.BlockSpec(
                (1, HALO, width), lambda b, i: (b, jnp.maximum(blk_of(i) * rb - 1, 0), col_blk)))
            specs.append(pl.BlockSpec(
                (1, HALO, width), lambda b, i: (b, jnp.minimum((blk_of(i) + 1) * rb, last), col_blk)))
        return specs

    return mk(lambda i: i), mk(lambda i: nblk - 1 - i)


def _mixer_out(bsz, seq, rows, nblk):
    shape = jax.ShapeDtypeStruct((bsz, seq, MIX_W), F32)
    specs = [pl.BlockSpec((1, rows, MIX_W), lambda b, i: (b, i, 0)),
             pl.BlockSpec((1, rows, MIX_W), lambda b, i: (b, nblk - 1 - i, 0))]
    return [shape, shape], specs


def _gdn_mixer(slab, conv_w, a_log, dt_bias):
    bsz, seq, _ = slab.shape
    rows = min(MIX_ROWS, seq)
    nblk = seq // rows
    nc = rows // CHUNK
    f_main, b_main = _bidir_specs(rows, nblk, GDN_CONV_CH, 0, True)
    small_blk = (GDN_CONV_CH + GDN_VW) // LANES
    f_small, _ = _bidir_specs(rows, nblk, LANES, small_blk, False)
    _, b_small = _bidir_specs(rows, nblk, LANES, small_blk + 1, False)
    convw = jnp.zeros((SUBLANES, GDN_CONV_CH), F32).at[:CONV_K].set(conv_w)
    prow = jnp.zeros((SUBLANES, LANES), F32)
    for d in range(2):
        prow = prow.at[2 * d, GDN_HEADS:2 * GDN_HEADS].set(-jnp.exp(a_log[d]))
        prow = prow.at[2 * d + 1, GDN_HEADS:2 * GDN_HEADS].set(dt_bias[d])
    out_shape, out_specs = _mixer_out(bsz, seq, rows, nblk)
    hd = (2, nc, GDN_HEADS)
    return pl.pallas_call(
        functools.partial(_gdn_kernel, rows=rows, nblk=nblk),
        out_shape=out_shape,
        grid=(bsz, nblk),
        in_specs=f_main + f_small + b_main + b_small + [
            _resident((SUBLANES, GDN_CONV_CH)), _resident((SUBLANES, LANES))],
        out_specs=out_specs,
        scratch_shapes=[
            pltpu.VMEM((2, rows + 2 * HALO, GDN_CONV_CH), F32),
            pltpu.VMEM(hd + (2 * CHUNK, GDN_DK), BF16),
            pltpu.VMEM(hd + (CHUNK, GDN_DV), F32),
            pltpu.VMEM(hd + (CHUNK, CHUNK), BF16),
            pltpu.VMEM(hd + (CHUNK, GDN_DK), BF16),
            pltpu.VMEM(hd + (SUBLANES, LANES), F32),
            pltpu.VMEM((2, GDN_HEADS, GDN_DK, GDN_DV), F32),
        ],
        compiler_params=pltpu.CompilerParams(
            dimension_semantics=("arbitrary", "arbitrary"), vmem_limit_bytes=VMEM_LIMIT),
        name="gdn",
    )(slab, slab, slab, slab, slab, slab, slab, slab, convw, prow)


GLA_LEVELS = 6


def _gla_tables(direction):
    pos = np.arange(CHUNK)
    order = pos if direction == 0 else CHUNK - 1 - pos
    wst = np.zeros((GLA_LEVELS + 1, CHUNK, CHUNK), np.float32)
    msk = np.zeros((GLA_LEVELS + 1, CHUNK, CHUNK), np.float32)
    for lvl in range(GLA_LEVELS):
        s = CHUNK >> (lvl + 1)
        blk = order // (2 * s)
        later = (order % (2 * s)) >= s
        bound = blk * 2 * s + s - 1
        for i in range(CHUNK):
            if later[i]:
                wst[lvl, i] = (order > bound[i]) & (order <= order[i])
            else:
                wst[lvl, i] = (order > order[i]) & (order <= bound[i])
        msk[lvl] = (blk[:, None] == blk[None, :]) & later[:, None] & (~later[None, :])
    wst[GLA_LEVELS] = order[None, :] <= order[:, None]
    msk[GLA_LEVELS] = np.eye(CHUNK)
    return wst.reshape((GLA_LEVELS + 1) * CHUNK, CHUNK), msk


def _gla_chunk(d, c, qkv, lr_ref, wst_ref, msk_ref, wg_ref, bg_ref, s_ref, o_ref):
    base = pl.multiple_of(c * CHUNK, CHUNK)
    rows = pl.ds(base, CHUNK)
    q = qkv[0, rows, 0:GLA_KW] * (GLA_DK ** -0.5)
    k = qkv[0, rows, GLA_KW:2 * GLA_KW]
    lr = lr_ref[0, rows, :]
    pre = _dot(_bf(lr), wg_ref[d]) + bg_ref[d:d + 1, :]
    gk = -_softplus(-pre) * (1.0 / GLA_TAU)
    e_all = _mask_dot(wst_ref[d], gk, n=2)
    att = [None] * GLA_HEADS
    for lvl in range(GLA_LEVELS + 1):
        if lvl < GLA_LEVELS:
            x = jnp.exp(e_all[lvl * CHUNK:(lvl + 1) * CHUNK])
            qx = _bf(q * x)
            kx = _bf(k * x)
        else:
            qx = _bf(q)
            kx = _bf(k)
        m = msk_ref[d, lvl]
        for h in range(GLA_HEADS):
            hs = slice(h * GLA_DK, (h + 1) * GLA_DK)
            p = m * _dot_nt(qx[:, hs], kx[:, hs])
            att[h] = p if att[h] is None else att[h] + p
    gcum = e_all[GLA_LEVELS * CHUNK:(GLA_LEVELS + 1) * CHUNK]
    last = CHUNK - 1 if d == 0 else 0
    g_last = gcum[last:last + 1, :]
    qg = _bf(q * jnp.exp(gcum))
    kg = _bf(k * jnp.exp(g_last - gcum))
    e_last = jnp.exp(g_last)
    for h in range(GLA_HEADS):
        hs = slice(h * GLA_DK, (h + 1) * GLA_DK)
        v = _bf(qkv[0, rows, 2 * GLA_KW + h * GLA_DV:2 * GLA_KW + (h + 1) * GLA_DV])
        st = s_ref[d, h]
        o = _dot_nt(qg[:, hs], _bf(st)) + _dot(_bf(att[h]), v)
        s_ref[d, h] = st * e_last[:, hs] + _dot_tn(v, kg[:, hs])
        o_ref[0, rows, h * GLA_DV:(h + 1) * GLA_DV] = o


def _gla_kernel(qkv_f, lr_f, qkv_b, lr_b, wst_ref, msk_ref, wg_ref, bg_ref,
                of_ref, ob_ref, s_ref, *, rows):
    i = pl.program_id(1)
    nc = rows // CHUNK

    @pl.when(i == 0)
    def _():
        s_ref[...] = jnp.zeros_like(s_ref)

    def body(j, carry):
        _gla_chunk(0, j, qkv_f, lr_f, wst_ref, msk_ref, wg_ref, bg_ref, s_ref, of_ref)
        _gla_chunk(1, nc - 1 - j, qkv_b, lr_b, wst_ref, msk_ref, wg_ref, bg_ref, s_ref, ob_ref)
        return carry

    lax.fori_loop(0, nc, body, 0)


def _gla_mixer(slab, w_gup, b_g):
    bsz, seq, _ = slab.shape
    rows = min(MIX_ROWS, seq)
    nblk = seq // rows
    qkv_w = 2 * GLA_KW + GLA_VW
    f_qkv, b_qkv = _bidir_specs(rows, nblk, qkv_w, 0, False)
    lr_blk = (2 * GLA_KW + 2 * GLA_VW) // LANES
    f_lr, b_lr = _bidir_specs(rows, nblk, LANES, lr_blk, False)
    tabs = [_gla_tables(d) for d in range(2)]
    wst = jnp.asarray(np.stack([t[0] for t in tabs]), BF16)
    msk = jnp.asarray(np.stack([t[1] for t in tabs]), F32)
    wg = jnp.zeros((2, LANES, GLA_KW), F32)
    for d in range(2):
        wg = wg.at[d, d * GLA_RANK:(d + 1) * GLA_RANK].set(w_gup[d])
    out_shape, out_specs = _mixer_out(bsz, seq, rows, nblk)
    return pl.pallas_call(
        functools.partial(_gla_kernel, rows=rows),
        out_shape=out_shape,
        grid=(bsz, nblk),
        in_specs=f_qkv + f_lr + b_qkv + b_lr + [
            _resident(wst.shape), _resident(msk.shape),
            _resident((2, LANES, GLA_KW)), _resident((2, GLA_KW))],
        out_specs=out_specs,
        scratch_shapes=[pltpu.VMEM((2, GLA_HEADS, GLA_DV, GLA_DK), F32)],
        compiler_params=pltpu.CompilerParams(
            dimension_semantics=("arbitrary", "arbitrary"), vmem_limit_bytes=VMEM_LIMIT),
        name="gla",
    )(slab, slab, slab, slab, wst, msk, wg.astype(BF16), b_g.astype(F32))


def _ssd_chunk(d, c, xp_ref, small, convw_ref, convb_ref, prow_ref, expand_ref, dskip_ref,
               s_ref, o_ref):
    incl, _ = _scan_masks(d)
    m_incl = jnp.where(incl, 1.0, 0.0).astype(BF16)
    base = pl.multiple_of(c * CHUNK, CHUNK)
    rows = pl.ds(base, CHUNK)

    def conv(lo, hi):
        cols = slice(lo, hi)
        win = xp_ref[d, pl.ds(base, CHUNK + 2 * HALO), cols]
        return _silu(_conv_window(win, convw_ref, cols, convb_ref[:, cols]))

    sm = small[0, rows, :]
    neg_a = prow_ref[2 * d:2 * d + 1, :]
    dtb = prow_ref[2 * d + 1:2 * d + 2, :]
    dt = _softplus(sm + dtb)
    acum = _mask_dot(m_incl, dt * neg_a)
    acum_t = acum.T
    expand = expand_ref[...]
    dt_x = _dot_mask(dt, expand)
    acum_x = _dot_mask(acum, expand)
    last = CHUNK - 1 if d == 0 else 0
    a_last = acum_x[last:last + 1, :]
    e_acum = jnp.exp(acum_x)
    e_tail = jnp.exp(a_last - acum_x)
    e_last = jnp.exp(a_last)
    gw = SSD_HPG * SSD_HEADDIM
    for g in range(SSD_GROUPS):
        gs = slice(g * gw, (g + 1) * gw)
        x = conv(g * gw, (g + 1) * gw)
        bm = conv(SSD_INNER + g * SSD_STATE, SSD_INNER + (g + 1) * SSD_STATE)
        cm = conv(SSD_INNER + SSD_BCW + g * SSD_STATE, SSD_INNER + SSD_BCW + (g + 1) * SSD_STATE)
        xdt = x * dt_x[:, gs]
        xdt_b = _bf(xdt)
        cb = _dot_nt(_bf(cm), _bf(bm))
        st = s_ref[d, g]
        y = _dot(_bf(cm), _bf(st)) * e_acum[:, gs]
        parts = []
        for hh in range(SSD_HPG):
            hd = g * SSD_HPG + hh
            diff = acum[:, hd:hd + 1] - acum_t[hd:hd + 1, :]
            lmat = jnp.exp(jnp.where(incl, diff, NEG_BIG))
            ps = slice(hh * SSD_HEADDIM, (hh + 1) * SSD_HEADDIM)
            parts.append(_dot(_bf(cb * lmat), xdt_b[:, ps]))
        y = y + jnp.concatenate(parts, axis=-1)
        if d == 0:
            y = y + x * dskip_ref[:, gs]
        s_ref[d, g] = st * e_last[:, gs] + _dot_tn(_bf(bm), _bf(xdt * e_tail[:, gs]))
        o_ref[0, rows, gs] = y


def _ssd_kernel(main_f, prev_f, next_f, small_f, main_b, prev_b, next_b, small_b,
                convw_ref, convb_ref, prow_ref, expand_ref, dskip_ref, of_ref, ob_ref,
                xp_ref, s_ref, *, rows, nblk):
    i = pl.program_id(1)
    nc = rows // CHUNK

    @pl.when(i == 0)
    def _():
        s_ref[...] = jnp.zeros_like(s_ref)

    _stage_halo(xp_ref, 0, main_f, prev_f, next_f, i, nblk, rows)
    _stage_halo(xp_ref, 1, main_b, prev_b, next_b, nblk - 1 - i, nblk, rows)

    def body(j, carry):
        _ssd_chunk(0, j, xp_ref, small_f, convw_ref, convb_ref, prow_ref, expand_ref, dskip_ref,
                   s_ref, of_ref)
        _ssd_chunk(1, nc - 1 - j, xp_ref, small_b, convw_ref, convb_ref, prow_ref, expand_ref,
                   dskip_ref, s_ref, ob_ref)
        return carry

    lax.fori_loop(0, nc, body, 0)


def _ssd_mixer(slab, conv_w, conv_b, a_log, dt_bias, d_skip):
    bsz, seq, _ = slab.shape
    rows = min(MIX_ROWS, seq)
    nblk = seq // rows
    f_main, b_main = _bidir_specs(rows, nblk, SSD_CONV_CH, 0, True)
    small_blk = (SSD_CONV_CH + SSD_INNER) // LANES
    f_small, _ = _bidir_specs(rows, nblk, LANES, small_blk, False)
    _, b_small = _bidir_specs(rows, nblk, LANES, small_blk + 1, False)
    convw = jnp.zeros((SUBLANES, SSD_CONV_CH), F32).at[:CONV_K].set(conv_w)
    convb = conv_b.reshape(1, SSD_CONV_CH).astype(F32)
    prow = jnp.zeros((SUBLANES, LANES), F32)
    for d in range(2):
        prow = prow.at[2 * d, 0:SSD_HEADS].set(-jnp.exp(a_log[d]))
        prow = prow.at[2 * d + 1, 0:SSD_HEADS].set(dt_bias[d])
    expand = np.zeros((LANES, SSD_INNER), np.float32)
    for hd in range(SSD_HEADS):
        expand[hd, hd * SSD_HEADDIM:(hd + 1) * SSD_HEADDIM] = 1.0
    dskip = jnp.repeat(d_skip.astype(F32), SSD_HEADDIM).reshape(1, SSD_INNER)
    out_shape, out_specs = _mixer_out(bsz, seq, rows, nblk)
    return pl.pallas_call(
        functools.partial(_ssd_kernel, rows=rows, nblk=nblk),
        out_shape=out_shape,
        grid=(bsz, nblk),
        in_specs=f_main + f_small + b_main + b_small + [
            _resident((SUBLANES, SSD_CONV_CH)), _resident((1, SSD_CONV_CH)),
            _resident((SUBLANES, LANES)), _resident((LANES, SSD_INNER)),
            _resident((1, SSD_INNER))],
        out_specs=out_specs,
        scratch_shapes=[
            pltpu.VMEM((2, rows + 2 * HALO, SSD_CONV_CH), F32),
            pltpu.VMEM((2, SSD_GROUPS, SSD_STATE, SSD_HPG * SSD_HEADDIM), F32),
        ],
        compiler_params=pltpu.CompilerParams(
            dimension_semantics=("arbitrary", "arbitrary"), vmem_limit_bytes=VMEM_LIMIT),
        name="ssd",
    )(slab, slab, slab, slab, slab, slab, slab, slab, convw, convb, prow,
      jnp.asarray(expand, BF16), dskip)


def _slab_columns():
    o = dict(zip(("a_qkv", "a_z", "a_b", "a_a", "b_q", "b_k", "b_v", "b_r", "b_g",
                  "c_z", "c_xbc", "c_dt", "gate"), IN_OFFS))
    pad = IN_COLS

    def rng(start, n):
        return list(range(start, start + n))

    def padded(cols):
        return cols + [pad] * (LANES - len(cols))

    gdn = rng(o["a_qkv"], GDN_CONV_CH) + rng(o["a_z"], GDN_VW)
    for d in range(2):
        gdn += padded(rng(o["a_b"] + d * GDN_HEADS, GDN_HEADS) + rng(o["a_a"] + d * GDN_HEADS, GDN_HEADS))
    gla = (rng(o["b_q"], GLA_KW) + rng(o["b_k"], GLA_KW) + rng(o["b_v"], GLA_VW)
           + rng(o["b_r"], GLA_VW) + padded(rng(o["b_g"], 2 * GLA_RANK)))
    ssd = rng(o["c_xbc"], SSD_CONV_CH) + rng(o["c_z"], SSD_INNER)
    for d in range(2):
        ssd += padded(rng(o["c_dt"] + d * SSD_HEADS, SSD_HEADS))
    gate = rng(o["gate"], GATE_SLAB)
    assert len(gdn) == GDN_SLAB and len(gla) == GLA_SLAB and len(ssd) == SSD_SLAB
    return tuple(np.asarray(c, np.int32) for c in (gdn, gla, ssd, gate))


_SLAB_COLS = _slab_columns()


def _layer(x, p, final_g):
    bsz, seq, _ = x.shape
    t = bsz * seq
    x2d = x.reshape(t, D_MODEL)
    x2d = _ffn(x2d, p["ffn_norm"][0], _bf(p["ffn_w_gate"][0]), _bf(p["ffn_w_up"][0]),
               _bf(p["ffn_w_down"][0]))
    w_ext = jnp.concatenate([p["w_in"], jnp.zeros((D_MODEL, 1), F32)], axis=1)
    slabs = [_inproj(x2d, p["mix_norm"], _bf(jnp.take(w_ext, cols, axis=1))) for cols in _SLAB_COLS]
    za, zb, zc, gates = slabs
    oa = _gdn_mixer(za.reshape(bsz, seq, GDN_SLAB), p["gdn_conv_w"], p["gdn_a_log"], p["gdn_dt_bias"])
    ob = _gla_mixer(zb.reshape(bsz, seq, GLA_SLAB), p["gla_w_gup"], p["gla_b_g"])
    oc = _ssd_mixer(zc.reshape(bsz, seq, SSD_SLAB), p["ssd_conv_w"], p["ssd_conv_b"],
                    p["ssd_a_log"], p["ssd_dt_bias"], p["ssd_d"])
    flat = lambda pair: [o.reshape(t, MIX_W) for o in pair]
    na = jnp.tile(p["gdn_norm"], GDN_HEADS).reshape(1, MIX_W)
    nb = jnp.tile(p["gla_norm"], GLA_HEADS).reshape(1, MIX_W)
    nc = p["ssd_norm"].reshape(1, MIX_W)
    x2d = _merge(x2d, flat(oa), za, flat(ob), zb, flat(oc), zc, gates, na, nb, nc,
                 _bf(p["w_branch"]), _bf(p["w_out"]))
    x2d = _ffn(x2d, p["ffn_norm"][1], _bf(p["ffn_w_gate"][1]), _bf(p["ffn_w_up"][1]),
               _bf(p["ffn_w_down"][1]), final_g)
    return x2d.reshape(bsz, seq, D_MODEL)


_LAYER_KEYS = ("ffn_norm", "ffn_w_gate", "ffn_w_up", "ffn_w_down", "mix_norm", "w_in",
               "gdn_conv_w", "gdn_a_log", "gdn_dt_bias", "gdn_norm", "gla_w_gup", "gla_b_g",
               "gla_norm", "ssd_conv_w", "ssd_conv_b", "ssd_a_log", "ssd_dt_bias", "ssd_d",
               "ssd_norm", "w_branch", "w_out")


def _trunk(x, stacked, final_norm):
    depth = stacked["w_in"].shape[0]
    for i in range(depth):
        p = {k: v[i] for k, v in stacked.items()}
        x = _layer(x, p, final_norm if i == depth - 1 else None)
    return x


def kernel(x_prompt, x_sample, ffn_norm, ffn_w_gate, ffn_w_up, ffn_w_down, mix_norm, w_in, gdn_conv_w, gdn_a_log, gdn_dt_bias, gdn_norm, gla_w_gup, gla_b_g, gla_norm, ssd_conv_w, ssd_conv_b, ssd_a_log, ssd_dt_bias, ssd_d, ssd_norm, w_branch, w_out, final_norm):
    stacked = dict(zip(_LAYER_KEYS, (
        ffn_norm, ffn_w_gate, ffn_w_up, ffn_w_down, mix_norm, w_in, gdn_conv_w, gdn_a_log,
        gdn_dt_bias, gdn_norm, gla_w_gup, gla_b_g, gla_norm, ssd_conv_w, ssd_conv_b, ssd_a_log,
        ssd_dt_bias, ssd_d, ssd_norm, w_branch, w_out)))
    assert x_prompt.shape[1:] == x_sample.shape[1:]
    nb = x_prompt.shape[0]
    y = _trunk(jnp.concatenate([x_prompt, x_sample], axis=0), stacked, final_norm)
    return y[:nb], y[nb:]
```

```python
import functools

import numpy as np
import jax
import jax.numpy as jnp
from jax import lax
from jax.experimental import pallas as pl
from jax.experimental.pallas import tpu as pltpu

F32 = jnp.float32
BF16 = jnp.bfloat16

D_MODEL = 1024
DEPTH = 4
CHUNK = 64
CONV_K = 5
NORM_EPS = 1e-6
D_FF = 2816
N_BRANCH = 3
MIX_W = 512

GDN_HEADS = 4
GDN_DK = 128
GDN_DV = 128
GDN_KW = GDN_HEADS * GDN_DK
GDN_VW = GDN_HEADS * GDN_DV
GDN_CONV_CH = 2 * GDN_KW + GDN_VW

GLA_HEADS = 4
GLA_DK = 64
GLA_DV = 128
GLA_KW = GLA_HEADS * GLA_DK
GLA_VW = GLA_HEADS * GLA_DV
GLA_RANK = 16
GLA_TAU = 16.0

SSD_HEADS = 8
SSD_HEADDIM = 64
SSD_GROUPS = 2
SSD_STATE = 128
SSD_HPG = SSD_HEADS // SSD_GROUPS
SSD_INNER = SSD_HEADS * SSD_HEADDIM
SSD_BCW = SSD_GROUPS * SSD_STATE
SSD_CONV_CH = SSD_INNER + 2 * SSD_BCW

IN_SIZES = (GDN_CONV_CH, GDN_VW, 2 * GDN_HEADS, 2 * GDN_HEADS,
            GLA_KW, GLA_KW, GLA_VW, GLA_VW, 2 * GLA_RANK,
            SSD_INNER, SSD_CONV_CH, 2 * SSD_HEADS,
            N_BRANCH * D_MODEL)
IN_COLS = sum(IN_SIZES)
IN_OFFS = tuple(int(v) for v in np.cumsum((0,) + IN_SIZES[:-1]))

LANES = 128
SUBLANES = 8
VMEM_LIMIT = 56 * 1024 * 1024

HALO = SUBLANES
TM_DENSE = 512
TM_MERGE = 256
FF_TILE = 256
MIX_ROWS = 256
NEG_BIG = -1e30

GDN_SLAB = GDN_CONV_CH + GDN_VW + 2 * LANES
GLA_SLAB = 2 * GLA_KW + 2 * GLA_VW + LANES
SSD_SLAB = SSD_INNER + SSD_CONV_CH + 2 * LANES
GATE_SLAB = N_BRANCH * D_MODEL


def _silu(x):
    return x * jax.nn.sigmoid(x)


def _softplus(x):
    return jnp.maximum(x, 0.0) + jnp.log(1.0 + jnp.exp(-jnp.abs(x)))


def _rms(x, g):
    ms = jnp.mean(x * x, axis=-1, keepdims=True)
    return x * lax.rsqrt(ms + NORM_EPS) * g


def _dot(a, b):
    return jnp.dot(a, b, preferred_element_type=F32)


def _dot_nt(a, b):
    return lax.dot_general(a, b, (((1,), (1,)), ((), ())), preferred_element_type=F32)


def _dot_tn(a, b):
    return lax.dot_general(a, b, (((0,), (0,)), ((), ())), preferred_element_type=F32)


def _bf(x):
    return x.astype(BF16)


def _split_terms(x, n):
    terms = []
    r = x
    for t in range(n):
        p = r.astype(BF16)
        terms.append(p)
        if t + 1 < n:
            r = r - p.astype(F32)
    return terms


def _mask_dot(m01, x, n=3):
    acc = None
    for p in _split_terms(x, n):
        t = _dot(m01, p)
        acc = t if acc is None else acc + t
    return acc


def _dot_mask(x, m01, n=3):
    acc = None
    for p in _split_terms(x, n):
        t = _dot(p, m01)
        acc = t if acc is None else acc + t
    return acc


def _dot3(a, b):
    ah, al = _split_terms(a, 2)
    bh, bl = _split_terms(b, 2)
    return _dot(ah, bh) + (_dot(ah, bl) + _dot(al, bh))


def _scan_masks(direction):
    ri = lax.broadcasted_iota(jnp.int32, (CHUNK, CHUNK), 0)
    ci = lax.broadcasted_iota(jnp.int32, (CHUNK, CHUNK), 1)
    if direction == 0:
        return ci <= ri, ci < ri
    return ci >= ri, ci > ri


def _unit_tri_inverse(a):
    ri = lax.broadcasted_iota(jnp.int32, (CHUNK, CHUNK), 0)
    ci = lax.broadcasted_iota(jnp.int32, (CHUNK, CHUNK), 1)
    eye = jnp.where(ri == ci, 1.0, 0.0).astype(F32)
    s1 = _dot3(a, a)
    s2 = _dot3(s1, s1)
    s3 = _dot3(s2, s2)
    s4 = _dot3(s3, s3)
    s5 = _dot3(s4, s4)
    p1 = _dot3(eye - a, eye + s1)
    p2 = _dot3(eye + s2, eye + s3)
    p3 = _dot3(eye + s4, eye + s5)
    return _dot3(_dot3(p1, p2), p3)


def _conv_window(win, w_ref, cols, bias=None):
    acc = None
    for j in range(CONV_K):
        lo = HALO - CONV_K // 2 + j
        t = w_ref[j:j + 1, cols] * win[lo:lo + CHUNK]
        acc = t if acc is None else acc + t
    if bias is not None:
        acc = acc + bias
    return acc


def _conv_rows(xp_ref, d, row0, w_ref, cols):
    acc = None
    for j in range(CONV_K):
        lo = row0 + HALO - CONV_K // 2 + j
        t = w_ref[j:j + 1, cols] * xp_ref[d, lo:lo + CHUNK, cols]
        acc = t if acc is None else acc + t
    return acc


def _stage_halo(xp_ref, d, main, prev, nxt, blk, nblk, rows):
    xp_ref[d, 0:HALO, :] = jnp.where(blk == 0, 0.0, prev[0])
    xp_ref[d, HALO:HALO + rows, :] = main[0]
    xp_ref[d, HALO + rows:rows + 2 * HALO, :] = jnp.where(blk == nblk - 1, 0.0, nxt[0])


def _ffn_kernel(x_ref, g_ref, wg_ref, wu_ref, wd_ref, *rest, final_norm):
    if final_norm:
        fg_ref, o_ref, acc_ref = rest
    else:
        o_ref, acc_ref = rest
    x = x_ref[...]
    h = _bf(_rms(x, g_ref[...]))
    for c in range(D_FF // FF_TILE):
        sl = slice(c * FF_TILE, (c + 1) * FF_TILE)
        gate = _dot(h, wg_ref[:, sl])
        up = _dot(h, wu_ref[:, sl])
        act = _bf(_silu(gate) * up)
        part = _dot(act, wd_ref[sl, :])
        if c == 0:
            acc_ref[...] = part
        else:
            acc_ref[...] += part
    y = x + 0.5 * acc_ref[...]
    if final_norm:
        y = _rms(y, fg_ref[...])
    o_ref[...] = y


def _resident(shape):
    return pl.BlockSpec(shape, lambda *_: (0,) * len(shape), pipeline_mode=pl.Buffered(1))


def _ffn(x2d, norm_g, wg, wu, wd, final_g=None):
    t = x2d.shape[0]
    tm = min(TM_DENSE, t)
    row = pl.BlockSpec((tm, D_MODEL), lambda i: (i, 0))
    in_specs = [row, _resident((1, D_MODEL)), _resident((D_MODEL, D_FF)),
                _resident((D_MODEL, D_FF)), _resident((D_FF, D_MODEL))]
    args = [x2d, norm_g.reshape(1, D_MODEL), wg, wu, wd]
    if final_g is not None:
        in_specs.append(_resident((1, D_MODEL)))
        args.append(final_g.reshape(1, D_MODEL))
    return pl.pallas_call(
        functools.partial(_ffn_kernel, final_norm=final_g is not None),
        out_shape=jax.ShapeDtypeStruct((t, D_MODEL), F32),
        grid=(t // tm,),
        in_specs=in_specs,
        out_specs=row,
        scratch_shapes=[pltpu.VMEM((tm, D_MODEL), F32)],
        compiler_params=pltpu.CompilerParams(
            dimension_semantics=("arbitrary",), vmem_limit_bytes=VMEM_LIMIT),
        name="ffn_final" if final_g is not None else "ffn",
    )(*args)


def _inproj_kernel(x_ref, g_ref, w_ref, o_ref):
    h = _bf(_rms(x_ref[...], g_ref[...]))
    o_ref[...] = _dot(h, w_ref[...])


def _inproj(x2d, norm_g, w):
    t = x2d.shape[0]
    n = w.shape[1]
    tm = min(TM_DENSE, t)
    return pl.pallas_call(
        _inproj_kernel,
        out_shape=jax.ShapeDtypeStruct((t, n), F32),
        grid=(t // tm,),
        in_specs=[pl.BlockSpec((tm, D_MODEL), lambda i: (i, 0)),
                  _resident((1, D_MODEL)), _resident((D_MODEL, n))],
        out_specs=pl.BlockSpec((tm, n), lambda i: (i, 0)),
        compiler_params=pltpu.CompilerParams(
            dimension_semantics=("arbitrary",), vmem_limit_bytes=VMEM_LIMIT),
        name="inproj",
    )(x2d, norm_g.reshape(1, D_MODEL), w)


def _head_rms(o, g_row, width):
    parts = []
    for s in range(0, o.shape[-1], width):
        blk = o[:, s:s + width]
        ms = jnp.mean(blk * blk, axis=-1, keepdims=True)
        parts.append(blk * lax.rsqrt(ms + NORM_EPS))
    return jnp.concatenate(parts, axis=-1) * g_row


def _merge_kernel(x_ref, oaf_ref, oab_ref, za_ref, obf_ref, obb_ref, rb_ref,
                  ocf_ref, ocb_ref, zc_ref, gate_ref, na_ref, nb_ref, nc_ref,
                  wbr_ref, wout_ref, o_ref):
    ya = _head_rms(oaf_ref[...] + oab_ref[...], na_ref[...], GDN_DV) * _silu(za_ref[...])
    yb = _head_rms(obf_ref[...] + obb_ref[...], nb_ref[...], GLA_DV) * _silu(rb_ref[...])
    yc = (ocf_ref[...] + ocb_ref[...]) * _silu(zc_ref[...])
    yc = _head_rms(yc, nc_ref[...], SSD_INNER // SSD_GROUPS)
    merged = None
    for b, y in enumerate((ya, yb, yc)):
        gate = jax.nn.sigmoid(gate_ref[:, b * D_MODEL:(b + 1) * D_MODEL])
        t = gate * _dot(_bf(y), wbr_ref[b])
        merged = t if merged is None else merged + t
    o_ref[...] = x_ref[...] + _dot(_bf(merged), wout_ref[...])


def _merge(x2d, oa, za_slab, ob, zb_slab, oc, zc_slab, gates, na, nb, nc, wbr, wout):
    t = x2d.shape[0]
    tm = min(TM_MERGE, t)

    def col(width, blk):
        return pl.BlockSpec((tm, width), lambda i: (i, blk))

    in_specs = [
        col(D_MODEL, 0),
        col(MIX_W, 0), col(MIX_W, 0), col(MIX_W, GDN_CONV_CH // MIX_W),
        col(MIX_W, 0), col(MIX_W, 0), col(MIX_W, (2 * GLA_KW + GLA_VW) // MIX_W),
        col(MIX_W, 0), col(MIX_W, 0), col(MIX_W, SSD_CONV_CH // MIX_W),
        col(GATE_SLAB, 0),
        _resident((1, MIX_W)), _resident((1, MIX_W)), _resident((1, MIX_W)),
        _resident((N_BRANCH, MIX_W, D_MODEL)), _resident((D_MODEL, D_MODEL)),
    ]
    return pl.pallas_call(
        _merge_kernel,
        out_shape=jax.ShapeDtypeStruct((t, D_MODEL), F32),
        grid=(t // tm,),
        in_specs=in_specs,
        out_specs=col(D_MODEL, 0),
        compiler_params=pltpu.CompilerParams(
            dimension_semantics=("arbitrary",), vmem_limit_bytes=VMEM_LIMIT),
        name="merge",
    )(x2d, oa[0], oa[1], za_slab, ob[0], ob[1], zb_slab, oc[0], oc[1], zc_slab, gates,
      na, nb, nc, wbr, wout)


GDN_PW = GDN_HEADS * CHUNK


def _lane_groups(cols, width):
    per_tile = LANES // width
    tiles = []
    for t0 in range(0, len(cols), per_tile):
        lane = lax.broadcasted_iota(jnp.int32, (CHUNK, LANES), 1)
        tile = jnp.broadcast_to(cols[t0], (CHUNK, LANES))
        for s in range(1, per_tile):
            tile = jnp.where(lane >= s * width, jnp.broadcast_to(cols[t0 + s], (CHUNK, LANES)), tile)
        tiles.append(tile)
    return jnp.concatenate(tiles, axis=-1) if len(tiles) > 1 else tiles[0]


def _block_diag(y, mask01):
    return jnp.concatenate([y] * (mask01.shape[0] // y.shape[0]), axis=0) * mask01


def _packed_dot3(pairs, bd_ref):
    split = {}

    def terms(a):
        if id(a) not in split:
            split[id(a)] = _split_terms(a, 2)
        return split[id(a)]

    ops = []
    for x, y in pairs:
        xh, xl = terms(x)
        yh, yl = terms(y)
        ops.append((xh, xl, _block_diag(yh, bd_ref[...]), _block_diag(yl, bd_ref[...])))
    return [_dot(xh, bh) + (_dot(xh, bl) + _dot(xl, bh)) for xh, xl, bh, bl in ops]


def _gdn_group_setup(d, c, xp_ref, small, convw_ref, prow_ref, bdk_ref,
                     r_ref, wq_ref, att_ref, kg_ref, gl_ref):
    base = c * CHUNK
    ri = lax.broadcasted_iota(jnp.int32, (CHUNK, GDN_PW), 0)
    ci = lax.broadcasted_iota(jnp.int32, (CHUNK, GDN_PW), 1) & (CHUNK - 1)
    incl_p = (ci <= ri) if d == 0 else (ci >= ri)
    strict_p = (ci < ri) if d == 0 else (ci > ri)
    incl, _ = _scan_masks(d)
    m_incl = jnp.where(incl, 1.0, 0.0).astype(BF16)

    sm = small[0, base:base + CHUNK, :]
    neg_a = prow_ref[2 * d:2 * d + 1, :]
    dtb = prow_ref[2 * d + 1:2 * d + 2, :]
    beta_all = jax.nn.sigmoid(sm)
    g_all = neg_a * _softplus(sm + dtb)
    gcum = _mask_dot(m_incl, g_all)
    gcum_t = gcum.T
    last = CHUNK - 1 if d == 0 else 0
    heads = range(GDN_HEADS)
    g_cols = [gcum[:, GDN_HEADS + h:GDN_HEADS + h + 1] for h in heads]
    g_col_p = _lane_groups(g_cols, CHUNK)
    g_row_p = jnp.concatenate(
        [gcum_t[GDN_HEADS + h:GDN_HEADS + h + 1, :] for h in heads], axis=-1)
    diff = g_col_p - g_row_p
    dec_strict = jnp.exp(jnp.where(strict_p, diff, NEG_BIG))
    dec_incl = jnp.exp(jnp.where(incl_p, diff, NEG_BIG))

    def conv(off):
        outs = []
        for h in heads:
            cols = slice(off + h * LANES, off + (h + 1) * LANES)
            outs.append(_silu(_conv_rows(xp_ref, d, base, convw_ref, cols)))
        return outs

    def l2n(t):
        return t * lax.rsqrt(jnp.sum(t * t, axis=-1, keepdims=True) + NORM_EPS)

    q = jnp.concatenate([l2n(t) * (GDN_DK ** -0.5) for t in conv(0)], axis=-1)
    k = jnp.concatenate([l2n(t) for t in conv(GDN_KW)], axis=-1)
    v = jnp.concatenate(conv(2 * GDN_KW), axis=-1)
    beta_x = _lane_groups([beta_all[:, h:h + 1] for h in heads], LANES)
    g_x = _lane_groups(g_cols, LANES)
    g_last_x = g_x[last:last + 1, :]
    e_g = jnp.exp(g_x)
    kb = k * beta_x
    k_bd = _block_diag(_bf(k), bdk_ref[...])
    a_mat = _dot_nt(_bf(kb), k_bd) * dec_strict
    att = _dot_nt(_bf(q), k_bd) * dec_incl
    r_ref[d, c, :, 0:GDN_KW] = kb * e_g
    r_ref[d, c, :, GDN_KW:GDN_KW + GDN_VW] = v * beta_x
    qg = _bf(q * e_g)
    kg = _bf(k * jnp.exp(g_last_x - g_x))
    e_last = jnp.exp(g_last_x)
    for h in heads:
        hs = slice(h * LANES, (h + 1) * LANES)
        wq_ref[d, c, h, CHUNK:2 * CHUNK, :] = qg[:, hs]
        att_ref[d, c, h] = _bf(att[:, h * CHUNK:(h + 1) * CHUNK])
        kg_ref[d, c, h] = kg[:, hs]
        gl_ref[d, c, h] = jnp.broadcast_to(e_last[:, hs], (SUBLANES, LANES))
    return a_mat


def _gdn_prep(nc, xp_ref, smalls, convw_ref, prow_ref, bda_ref, bdk_ref, bdr_ref,
              r_ref, wq_ref, u_ref, att_ref, kg_ref, gl_ref):
    groups = [(d, c) for c in range(nc) for d in range(2)]
    a = [_gdn_group_setup(d, c, xp_ref, smalls[d], convw_ref, prow_ref, bdk_ref,
                          r_ref, wq_ref, att_ref, kg_ref, gl_ref) for d, c in groups]
    n = len(groups)
    ri = lax.broadcasted_iota(jnp.int32, (CHUNK, GDN_PW), 0)
    ci = lax.broadcasted_iota(jnp.int32, (CHUNK, GDN_PW), 1) & (CHUNK - 1)
    eye = jnp.where(ri == ci, 1.0, 0.0).astype(F32)
    s1 = _packed_dot3([(x, x) for x in a], bda_ref)
    lvl = _packed_dot3([(x, x) for x in s1] + [(eye - x, eye + y) for x, y in zip(a, s1)], bda_ref)
    s2, p1 = lvl[:n], lvl[n:]
    s3 = _packed_dot3([(x, x) for x in s2], bda_ref)
    lvl = _packed_dot3([(x, x) for x in s3] + [(eye + x, eye + y) for x, y in zip(s2, s3)], bda_ref)
    s4, p2 = lvl[:n], lvl[n:]
    lvl = _packed_dot3([(x, x) for x in s4] + list(zip(p1, p2)), bda_ref)
    s5, p12 = lvl[:n], lvl[n:]
    p3 = _packed_dot3([(eye + x, eye + y) for x, y in zip(s4, s5)], bda_ref)
    t_inv = _packed_dot3(list(zip(p12, p3)), bda_ref)
    sol = _packed_dot3([(t, r_ref[d, c]) for t, (d, c) in zip(t_inv, groups)], bdr_ref)
    for (d, c), wu in zip(groups, sol):
        for h in range(GDN_HEADS):
            wq_ref[d, c, h, 0:CHUNK, :] = _bf(wu[:, h * LANES:(h + 1) * LANES])
            u_ref[d, c, h] = wu[:, GDN_KW + h * LANES:GDN_KW + (h + 1) * LANES]


def _gdn_scan_step(units, wq_ref, u_ref, att_ref, kg_ref, gl_ref, s_ref):
    states = [s_ref[d, h] for d, _, h, _ in units]
    r = [_dot(wq_ref[d, c, h], _bf(s)) for (d, c, h, _), s in zip(units, states)]
    vb = [_bf(u_ref[d, c, h] - x[0:CHUNK]) for (d, c, h, _), x in zip(units, r)]
    for (d, c, h, o_ref), s, x, v in zip(units, states, r, vb):
        base = pl.multiple_of(c * CHUNK, CHUNK)
        o_ref[0, pl.ds(base, CHUNK), h * LANES:(h + 1) * LANES] = (
            x[CHUNK:2 * CHUNK] + _dot(att_ref[d, c, h], v))
        s_ref[d, h] = s * gl_ref[d, c, h, 0:1, :] + _dot_tn(kg_ref[d, c, h], v)


def _gdn_kernel(main_f, prev_f, next_f, small_f, main_b, prev_b, next_b, small_b,
                convw_ref, prow_ref, bda_ref, bdk_ref, bdr_ref, of_ref, ob_ref,
                xp_ref, r_ref, wq_ref, u_ref, att_ref, kg_ref, gl_ref, s_ref, *, rows, nblk):
    i = pl.program_id(1)
    nc = rows // CHUNK

    @pl.when(i == 0)
    def _():
        s_ref[...] = jnp.zeros_like(s_ref)

    _stage_halo(xp_ref, 0, main_f, prev_f, next_f, i, nblk, rows)
    _stage_halo(xp_ref, 1, main_b, prev_b, next_b, nblk - 1 - i, nblk, rows)
    _gdn_prep(nc, xp_ref, (small_f, small_b), convw_ref, prow_ref, bda_ref, bdk_ref, bdr_ref,
              r_ref, wq_ref, u_ref, att_ref, kg_ref, gl_ref)

    def scan(j, carry):
        units = [(0, j, h, of_ref) for h in range(GDN_HEADS)]
        units += [(1, nc - 1 - j, h, ob_ref) for h in range(GDN_HEADS)]
        _gdn_scan_step(units, wq_ref, u_ref, att_ref, kg_ref, gl_ref, s_ref)
        return carry

    lax.fori_loop(0, nc, scan, 0)


def _bidir_specs(rows, nblk, width, col_blk, halo):
    rb = rows // HALO
    last = nblk * rb - 1

    def mk(blk_of):
        specs = [pl.BlockSpec((1, rows, width), lambda b, i: (b, blk_of(i), col_blk))]
        if halo:
            specs.append(pl.BlockSpec(
                (1, HALO, width), lambda b, i: (b, jnp.maximum(blk_of(i) * rb - 1, 0), col_blk)))
            specs.append(pl.BlockSpec(
                (1, HALO, width), lambda b, i: (b, jnp.minimum((blk_of(i) + 1) * rb, last), col_blk)))
        return specs

    return mk(lambda i: i), mk(lambda i: nblk - 1 - i)


def _mixer_out(bsz, seq, rows, nblk):
    shape = jax.ShapeDtypeStruct((bsz, seq, MIX_W), F32)
    specs = [pl.BlockSpec((1, rows, MIX_W), lambda b, i: (b, i, 0)),
             pl.BlockSpec((1, rows, MIX_W), lambda b, i: (b, nblk - 1 - i, 0))]
    return [shape, shape], specs


def _gdn_mixer(slab, conv_w, a_log, dt_bias):
    bsz, seq, _ = slab.shape
    rows = min(MIX_ROWS, seq)
    nblk = seq // rows
    nc = rows // CHUNK
    f_main, b_main = _bidir_specs(rows, nblk, GDN_CONV_CH, 0, True)
    small_blk = (GDN_CONV_CH + GDN_VW) // LANES
    f_small, _ = _bidir_specs(rows, nblk, LANES, small_blk, False)
    _, b_small = _bidir_specs(rows, nblk, LANES, small_blk + 1, False)
    convw = jnp.zeros((SUBLANES, GDN_CONV_CH), F32).at[:CONV_K].set(conv_w)
    prow = jnp.zeros((SUBLANES, LANES), F32)
    for d in range(2):
        prow = prow.at[2 * d, GDN_HEADS:2 * GDN_HEADS].set(-jnp.exp(a_log[d]))
        prow = prow.at[2 * d + 1, GDN_HEADS:2 * GDN_HEADS].set(dt_bias[d])
    out_shape, out_specs = _mixer_out(bsz, seq, rows, nblk)
    hd = (2, nc, GDN_HEADS)
    row_head = np.arange(GDN_PW)[:, None] // CHUNK
    bda = jnp.asarray(row_head == np.arange(GDN_PW)[None, :] // CHUNK, BF16)
    bdk = jnp.asarray(row_head == np.arange(GDN_KW)[None, :] // GDN_DK, BF16)
    bdr = jnp.asarray(row_head == (np.arange(GDN_KW + GDN_VW)[None, :] % GDN_KW) // GDN_DK, BF16)
    return pl.pallas_call(
        functools.partial(_gdn_kernel, rows=rows, nblk=nblk),
        out_shape=out_shape,
        grid=(bsz, nblk),
        in_specs=f_main + f_small + b_main + b_small + [
            _resident((SUBLANES, GDN_CONV_CH)), _resident((SUBLANES, LANES)),
            _resident(bda.shape), _resident(bdk.shape), _resident(bdr.shape)],
        out_specs=out_specs,
        scratch_shapes=[
            pltpu.VMEM((2, rows + 2 * HALO, GDN_CONV_CH), F32),
            pltpu.VMEM((2, nc, CHUNK, GDN_KW + GDN_VW), F32),
            pltpu.VMEM(hd + (2 * CHUNK, GDN_DK), BF16),
            pltpu.VMEM(hd + (CHUNK, GDN_DV), F32),
            pltpu.VMEM(hd + (CHUNK, CHUNK), BF16),
            pltpu.VMEM(hd + (CHUNK, GDN_DK), BF16),
            pltpu.VMEM(hd + (SUBLANES, LANES), F32),
            pltpu.VMEM((2, GDN_HEADS, GDN_DK, GDN_DV), F32),
        ],
        compiler_params=pltpu.CompilerParams(
            dimension_semantics=("arbitrary", "arbitrary"), vmem_limit_bytes=VMEM_LIMIT),
        name="gdn",
    )(slab, slab, slab, slab, slab, slab, slab, slab, convw, prow, bda, bdk, bdr)


GLA_LEVELS = 6


def _gla_tables(direction):
    pos = np.arange(CHUNK)
    order = pos if direction == 0 else CHUNK - 1 - pos
    wst = np.zeros((GLA_LEVELS + 1, CHUNK, CHUNK), np.float32)
    msk = np.zeros((GLA_LEVELS + 1, CHUNK, CHUNK), np.float32)
    for lvl in range(GLA_LEVELS):
        s = CHUNK >> (lvl + 1)
        blk = order // (2 * s)
        later = (order % (2 * s)) >= s
        bound = blk * 2 * s + s - 1
        for i in range(CHUNK):
            if later[i]:
                wst[lvl, i] = (order > bound[i]) & (order <= order[i])
            else:
                wst[lvl, i] = (order > order[i]) & (order <= bound[i])
        msk[lvl] = (blk[:, None] == blk[None, :]) & later[:, None] & (~later[None, :])
    wst[GLA_LEVELS] = order[None, :] <= order[:, None]
    msk[GLA_LEVELS] = np.eye(CHUNK)
    return wst.reshape((GLA_LEVELS + 1) * CHUNK, CHUNK), msk


def _gla_chunks(groups, wst_ref, msk_ref, bdk_ref, bdv_ref, wg_ref, bg_ref, s_ref):
    n = len(groups)
    rows = [pl.ds(pl.multiple_of(c * CHUNK, CHUNK), CHUNK) for _, c, _, _, _ in groups]
    q = [qkv[0, r, 0:GLA_KW] * (GLA_DK ** -0.5) for (_, _, qkv, _, _), r in zip(groups, rows)]
    k = [qkv[0, r, GLA_KW:2 * GLA_KW] for (_, _, qkv, _, _), r in zip(groups, rows)]
    pre = [_dot(_bf(lr_ref[0, r, :]), wg_ref[d]) + bg_ref[d:d + 1, :]
           for (d, _, _, lr_ref, _), r in zip(groups, rows)]
    gk = [-_softplus(-p) * (1.0 / GLA_TAU) for p in pre]
    e_all = [_mask_dot(wst_ref[d], g, n=2) for (d, _, _, _, _), g in zip(groups, gk)]
    att = [None] * n
    for lvl in range(GLA_LEVELS + 1):
        for i, (d, _, _, _, _) in enumerate(groups):
            if lvl < GLA_LEVELS:
                x = jnp.exp(e_all[i][lvl * CHUNK:(lvl + 1) * CHUNK])
                qx, kx = _bf(q[i] * x), _bf(k[i] * x)
            else:
                qx, kx = _bf(q[i]), _bf(k[i])
            p = msk_ref[d, lvl] * _dot_nt(qx, _block_diag(kx, bdk_ref[...]))
            att[i] = p if att[i] is None else att[i] + p
    for i, (d, _, qkv, _, o_ref) in enumerate(groups):
        gcum = e_all[i][GLA_LEVELS * CHUNK:(GLA_LEVELS + 1) * CHUNK]
        last = CHUNK - 1 if d == 0 else 0
        g_last = gcum[last:last + 1, :]
        qg = _bf(q[i] * jnp.exp(gcum))
        kg = _bf(k[i] * jnp.exp(g_last - gcum))
        e_last = jnp.exp(g_last)
        v = _bf(qkv[0, rows[i], 2 * GLA_KW:2 * GLA_KW + GLA_VW])
        o_intra = _dot(_bf(att[i]), _block_diag(v, bdv_ref[...]))
        for h in range(GLA_HEADS):
            hs = slice(h * GLA_DK, (h + 1) * GLA_DK)
            vs = slice(h * GLA_DV, (h + 1) * GLA_DV)
            st = s_ref[d, h]
            o_ref[0, rows[i], vs] = o_intra[:, vs] + _dot_nt(qg[:, hs], _bf(st))
            s_ref[d, h] = st * e_last[:, hs] + _dot_tn(v[:, vs], kg[:, hs])


def _gla_kernel(qkv_f, lr_f, qkv_b, lr_b, wst_ref, msk_ref, bdk_ref, bdv_ref, wg_ref, bg_ref,
                of_ref, ob_ref, s_ref, *, rows):
    i = pl.program_id(1)
    nc = rows // CHUNK

    @pl.when(i == 0)
    def _():
        s_ref[...] = jnp.zeros_like(s_ref)

    def body(j, carry):
        groups = [(0, j, qkv_f, lr_f, of_ref), (1, nc - 1 - j, qkv_b, lr_b, ob_ref)]
        _gla_chunks(groups, wst_ref, msk_ref, bdk_ref, bdv_ref, wg_ref, bg_ref, s_ref)
        return carry

    lax.fori_loop(0, nc, body, 0)


def _gla_mixer(slab, w_gup, b_g):
    bsz, seq, _ = slab.shape
    rows = min(MIX_ROWS, seq)
    nblk = seq // rows
    qkv_w = 2 * GLA_KW + GLA_VW
    f_qkv, b_qkv = _bidir_specs(rows, nblk, qkv_w, 0, False)
    lr_blk = (2 * GLA_KW + 2 * GLA_VW) // LANES
    f_lr, b_lr = _bidir_specs(rows, nblk, LANES, lr_blk, False)
    tabs = [_gla_tables(d) for d in range(2)]
    wst = jnp.asarray(np.stack([t[0] for t in tabs]), BF16)
    msk = jnp.asarray(np.tile(np.stack([t[1] for t in tabs]), (1, 1, 1, GLA_HEADS)), F32)
    row_head = np.arange(GLA_HEADS * CHUNK)[:, None] // CHUNK
    bdk = jnp.asarray(row_head == np.arange(GLA_KW)[None, :] // GLA_DK, BF16)
    bdv = jnp.asarray(row_head == np.arange(GLA_VW)[None, :] // GLA_DV, BF16)
    wg = jnp.zeros((2, LANES, GLA_KW), F32)
    for d in range(2):
        wg = wg.at[d, d * GLA_RANK:(d + 1) * GLA_RANK].set(w_gup[d])
    out_shape, out_specs = _mixer_out(bsz, seq, rows, nblk)
    return pl.pallas_call(
        functools.partial(_gla_kernel, rows=rows),
        out_shape=out_shape,
        grid=(bsz, nblk),
        in_specs=f_qkv + f_lr + b_qkv + b_lr + [
            _resident(wst.shape), _resident(msk.shape), _resident(bdk.shape), _resident(bdv.shape),
            _resident((2, LANES, GLA_KW)), _resident((2, GLA_KW))],
        out_specs=out_specs,
        scratch_shapes=[pltpu.VMEM((2, GLA_HEADS, GLA_DV, GLA_DK), F32)],
        compiler_params=pltpu.CompilerParams(
            dimension_semantics=("arbitrary", "arbitrary"), vmem_limit_bytes=VMEM_LIMIT),
        name="gla",
    )(slab, slab, slab, slab, wst, msk, bdk, bdv, wg.astype(BF16), b_g.astype(F32))


def _ssd_chunk(d, c, xp_ref, small, convw_ref, convb_ref, prow_ref, expand_ref, dskip_ref,
               s_ref, o_ref):
    incl, _ = _scan_masks(d)
    m_incl = jnp.where(incl, 1.0, 0.0).astype(BF16)
    base = pl.multiple_of(c * CHUNK, CHUNK)
    rows = pl.ds(base, CHUNK)

    def conv(lo, hi):
        cols = slice(lo, hi)
        win = xp_ref[d, pl.ds(base, CHUNK + 2 * HALO), cols]
        return _silu(_conv_window(win, convw_ref, cols, convb_ref[:, cols]))

    sm = small[0, rows, :]
    neg_a = prow_ref[2 * d:2 * d + 1, :]
    dtb = prow_ref[2 * d + 1:2 * d + 2, :]
    dt = _softplus(sm + dtb)
    acum = _mask_dot(m_incl, dt * neg_a)
    acum_t = acum.T
    expand = expand_ref[...]
    dt_x = _dot_mask(dt, expand)
    acum_x = _dot_mask(acum, expand)
    last = CHUNK - 1 if d == 0 else 0
    a_last = acum_x[last:last + 1, :]
    e_acum = jnp.exp(acum_x)
    e_tail = jnp.exp(a_last - acum_x)
    e_last = jnp.exp(a_last)
    gw = SSD_HPG * SSD_HEADDIM
    for g in range(SSD_GROUPS):
        gs = slice(g * gw, (g + 1) * gw)
        x = conv(g * gw, (g + 1) * gw)
        bm = conv(SSD_INNER + g * SSD_STATE, SSD_INNER + (g + 1) * SSD_STATE)
        cm = conv(SSD_INNER + SSD_BCW + g * SSD_STATE, SSD_INNER + SSD_BCW + (g + 1) * SSD_STATE)
        xdt = x * dt_x[:, gs]
        xdt_b = _bf(xdt)
        cb = _dot_nt(_bf(cm), _bf(bm))
        st = s_ref[d, g]
        y = _dot(_bf(cm), _bf(st)) * e_acum[:, gs]
        parts = []
        for hh in range(SSD_HPG):
            hd = g * SSD_HPG + hh
            diff = acum[:, hd:hd + 1] - acum_t[hd:hd + 1, :]
            lmat = jnp.exp(jnp.where(incl, diff, NEG_BIG))
            ps = slice(hh * SSD_HEADDIM, (hh + 1) * SSD_HEADDIM)
            parts.append(_dot(_bf(cb * lmat), xdt_b[:, ps]))
        y = y + jnp.concatenate(parts, axis=-1)
        if d == 0:
            y = y + x * dskip_ref[:, gs]
        s_ref[d, g] = st * e_last[:, gs] + _dot_tn(_bf(bm), _bf(xdt * e_tail[:, gs]))
        o_ref[0, rows, gs] = y


def _ssd_kernel(main_f, prev_f, next_f, small_f, main_b, prev_b, next_b, small_b,
                convw_ref, convb_ref, prow_ref, expand_ref, dskip_ref, of_ref, ob_ref,
                xp_ref, s_ref, *, rows, nblk):
    i = pl.program_id(1)
    nc = rows // CHUNK

    @pl.when(i == 0)
    def _():
        s_ref[...] = jnp.zeros_like(s_ref)

    _stage_halo(xp_ref, 0, main_f, prev_f, next_f, i, nblk, rows)
    _stage_halo(xp_ref, 1, main_b, prev_b, next_b, nblk - 1 - i, nblk, rows)

    def body(j, carry):
        _ssd_chunk(0, j, xp_ref, small_f, convw_ref, convb_ref, prow_ref, expand_ref, dskip_ref,
                   s_ref, of_ref)
        _ssd_chunk(1, nc - 1 - j, xp_ref, small_b, convw_ref, convb_ref, prow_ref, expand_ref,
                   dskip_ref, s_ref, ob_ref)
        return carry

    lax.fori_loop(0, nc, body, 0)


def _ssd_mixer(slab, conv_w, conv_b, a_log, dt_bias, d_skip):
    bsz, seq, _ = slab.shape
    rows = min(MIX_ROWS, seq)
    nblk = seq // rows
    f_main, b_main = _bidir_specs(rows, nblk, SSD_CONV_CH, 0, True)
    small_blk = (SSD_CONV_CH + SSD_INNER) // LANES
    f_small, _ = _bidir_specs(rows, nblk, LANES, small_blk, False)
    _, b_small = _bidir_specs(rows, nblk, LANES, small_blk + 1, False)
    convw = jnp.zeros((SUBLANES, SSD_CONV_CH), F32).at[:CONV_K].set(conv_w)
    convb = conv_b.reshape(1, SSD_CONV_CH).astype(F32)
    prow = jnp.zeros((SUBLANES, LANES), F32)
    for d in range(2):
        prow = prow.at[2 * d, 0:SSD_HEADS].set(-jnp.exp(a_log[d]))
        prow = prow.at[2 * d + 1, 0:SSD_HEADS].set(dt_bias[d])
    expand = np.zeros((LANES, SSD_INNER), np.float32)
    for hd in range(SSD_HEADS):
        expand[hd, hd * SSD_HEADDIM:(hd + 1) * SSD_HEADDIM] = 1.0
    dskip = jnp.repeat(d_skip.astype(F32), SSD_HEADDIM).reshape(1, SSD_INNER)
    out_shape, out_specs = _mixer_out(bsz, seq, rows, nblk)
    return pl.pallas_call(
        functools.partial(_ssd_kernel, rows=rows, nblk=nblk),
        out_shape=out_shape,
        grid=(bsz, nblk),
        in_specs=f_main + f_small + b_main + b_small + [
            _resident((SUBLANES, SSD_CONV_CH)), _resident((1, SSD_CONV_CH)),
            _resident((SUBLANES, LANES)), _resident((LANES, SSD_INNER)),
            _resident((1, SSD_INNER))],
        out_specs=out_specs,
        scratch_shapes=[
            pltpu.VMEM((2, rows + 2 * HALO, SSD_CONV_CH), F32),
            pltpu.VMEM((2, SSD_GROUPS, SSD_STATE, SSD_HPG * SSD_HEADDIM), F32),
        ],
        compiler_params=pltpu.CompilerParams(
            dimension_semantics=("arbitrary", "arbitrary"), vmem_limit_bytes=VMEM_LIMIT),
        name="ssd",
    )(slab, slab, slab, slab, slab, slab, slab, slab, convw, convb, prow,
      jnp.asarray(expand, BF16), dskip)


def _slab_columns():
    o = dict(zip(("a_qkv", "a_z", "a_b", "a_a", "b_q", "b_k", "b_v", "b_r", "b_g",
                  "c_z", "c_xbc", "c_dt", "gate"), IN_OFFS))
    pad = IN_COLS

    def rng(start, n):
        return list(range(start, start + n))

    def padded(cols):
        return cols + [pad] * (LANES - len(cols))

    gdn = rng(o["a_qkv"], GDN_CONV_CH) + rng(o["a_z"], GDN_VW)
    for d in range(2):
        gdn += padded(rng(o["a_b"] + d * GDN_HEADS, GDN_HEADS) + rng(o["a_a"] + d * GDN_HEADS, GDN_HEADS))
    gla = (rng(o["b_q"], GLA_KW) + rng(o["b_k"], GLA_KW) + rng(o["b_v"], GLA_VW)
           + rng(o["b_r"], GLA_VW) + padded(rng(o["b_g"], 2 * GLA_RANK)))
    ssd = rng(o["c_xbc"], SSD_CONV_CH) + rng(o["c_z"], SSD_INNER)
    for d in range(2):
        ssd += padded(rng(o["c_dt"] + d * SSD_HEADS, SSD_HEADS))
    gate = rng(o["gate"], GATE_SLAB)
    assert len(gdn) == GDN_SLAB and len(gla) == GLA_SLAB and len(ssd) == SSD_SLAB
    return tuple(np.asarray(c, np.int32) for c in (gdn, gla, ssd, gate))


_SLAB_COLS = _slab_columns()


def _layer(x, p, final_g):
    bsz, seq, _ = x.shape
    t = bsz * seq
    x2d = x.reshape(t, D_MODEL)
    x2d = _ffn(x2d, p["ffn_norm"][0], _bf(p["ffn_w_gate"][0]), _bf(p["ffn_w_up"][0]),
               _bf(p["ffn_w_down"][0]))
    w_ext = jnp.concatenate([p["w_in"], jnp.zeros((D_MODEL, 1), F32)], axis=1)
    slabs = [_inproj(x2d, p["mix_norm"], _bf(jnp.take(w_ext, cols, axis=1))) for cols in _SLAB_COLS]
    za, zb, zc, gates = slabs
    oa = _gdn_mixer(za.reshape(bsz, seq, GDN_SLAB), p["gdn_conv_w"], p["gdn_a_log"], p["gdn_dt_bias"])
    ob = _gla_mixer(zb.reshape(bsz, seq, GLA_SLAB), p["gla_w_gup"], p["gla_b_g"])
    oc = _ssd_mixer(zc.reshape(bsz, seq, SSD_SLAB), p["ssd_conv_w"], p["ssd_conv_b"],
                    p["ssd_a_log"], p["ssd_dt_bias"], p["ssd_d"])
    flat = lambda pair: [o.reshape(t, MIX_W) for o in pair]
    na = jnp.tile(p["gdn_norm"], GDN_HEADS).reshape(1, MIX_W)
    nb = jnp.tile(p["gla_norm"], GLA_HEADS).reshape(1, MIX_W)
    nc = p["ssd_norm"].reshape(1, MIX_W)
    x2d = _merge(x2d, flat(oa), za, flat(ob), zb, flat(oc), zc, gates, na, nb, nc,
                 _bf(p["w_branch"]), _bf(p["w_out"]))
    x2d = _ffn(x2d, p["ffn_norm"][1], _bf(p["ffn_w_gate"][1]), _bf(p["ffn_w_up"][1]),
               _bf(p["ffn_w_down"][1]), final_g)
    return x2d.reshape(bsz, seq, D_MODEL)


_LAYER_KEYS = ("ffn_norm", "ffn_w_gate", "ffn_w_up", "ffn_w_down", "mix_norm", "w_in",
               "gdn_conv_w", "gdn_a_log", "gdn_dt_bias", "gdn_norm", "gla_w_gup", "gla_b_g",
               "gla_norm", "ssd_conv_w", "ssd_conv_b", "ssd_a_log", "ssd_dt_bias", "ssd_d",
               "ssd_norm", "w_branch", "w_out")


def _trunk(x, stacked, final_norm):
    depth = stacked["w_in"].shape[0]
    for i in range(depth):
        p = {k: v[i] for k, v in stacked.items()}
        x = _layer(x, p, final_norm if i == depth - 1 else None)
    return x


def kernel(x_prompt, x_sample, ffn_norm, ffn_w_gate, ffn_w_up, ffn_w_down, mix_norm, w_in, gdn_conv_w, gdn_a_log, gdn_dt_bias, gdn_norm, gla_w_gup, gla_b_g, gla_norm, ssd_conv_w, ssd_conv_b, ssd_a_log, ssd_dt_bias, ssd_d, ssd_norm, w_branch, w_out, final_norm):
    stacked = dict(zip(_LAYER_KEYS, (
        ffn_norm, ffn_w_gate, ffn_w_up, ffn_w_down, mix_norm, w_in, gdn_conv_w, gdn_a_log,
        gdn_dt_bias, gdn_norm, gla_w_gup, gla_b_g, gla_norm, ssd_conv_w, ssd_conv_b, ssd_a_log,
        ssd_dt_bias, ssd_d, ssd_norm, w_branch, w_out)))
    assert x_prompt.shape[1:] == x_sample.shape[1:]
    nb = x_prompt.shape[0]
    y = _trunk(jnp.concatenate([x_prompt, x_sample], axis=0), stacked, final_norm)
    return y[:nb], y[nb:]
```

```python
import functools

import numpy as np
import jax
import jax.numpy as jnp
from jax import lax
from jax.experimental import pallas as pl
from jax.experimental.pallas import tpu as pltpu

F32 = jnp.float32
BF16 = jnp.bfloat16

D_MODEL = 1024
DEPTH = 4
CHUNK = 64
CONV_K = 5
NORM_EPS = 1e-6
D_FF = 2816
N_BRANCH = 3
MIX_W = 512

GDN_HEADS = 4
GDN_DK = 128
GDN_DV = 128
GDN_KW = GDN_HEADS * GDN_DK
GDN_VW = GDN_HEADS * GDN_DV
GDN_CONV_CH = 2 * GDN_KW + GDN_VW

GLA_HEADS = 4
GLA_DK = 64
GLA_DV = 128
GLA_KW = GLA_HEADS * GLA_DK
GLA_VW = GLA_HEADS * GLA_DV
GLA_RANK = 16
GLA_TAU = 16.0

SSD_HEADS = 8
SSD_HEADDIM = 64
SSD_GROUPS = 2
SSD_STATE = 128
SSD_HPG = SSD_HEADS // SSD_GROUPS
SSD_INNER = SSD_HEADS * SSD_HEADDIM
SSD_BCW = SSD_GROUPS * SSD_STATE
SSD_CONV_CH = SSD_INNER + 2 * SSD_BCW

IN_SIZES = (GDN_CONV_CH, GDN_VW, 2 * GDN_HEADS, 2 * GDN_HEADS,
            GLA_KW, GLA_KW, GLA_VW, GLA_VW, 2 * GLA_RANK,
            SSD_INNER, SSD_CONV_CH, 2 * SSD_HEADS,
            N_BRANCH * D_MODEL)
IN_COLS = sum(IN_SIZES)
IN_OFFS = tuple(int(v) for v in np.cumsum((0,) + IN_SIZES[:-1]))

LANES = 128
SUBLANES = 8
VMEM_LIMIT = 56 * 1024 * 1024

HALO = SUBLANES
TM_DENSE = 512
TM_MERGE = 256
FF_TILE = 256
MIX_ROWS = 256
LOOP_UNROLL = 2
NEG_BIG = -1e30

GDN_SLAB = GDN_CONV_CH + GDN_VW + 2 * LANES
GLA_SLAB = 2 * GLA_KW + 2 * GLA_VW + LANES
SSD_SLAB = SSD_INNER + SSD_CONV_CH + 2 * LANES
GATE_SLAB = N_BRANCH * D_MODEL


def _silu(x):
    return x * jax.nn.sigmoid(x)


def _softplus(x):
    return jnp.maximum(x, 0.0) + jnp.log(1.0 + jnp.exp(-jnp.abs(x)))


def _rms(x, g):
    ms = jnp.mean(x * x, axis=-1, keepdims=True)
    return x * lax.rsqrt(ms + NORM_EPS) * g


def _dot(a, b):
    return jnp.dot(a, b, preferred_element_type=F32)


def _dot_nt(a, b):
    return lax.dot_general(a, b, (((1,), (1,)), ((), ())), preferred_element_type=F32)


def _dot_tn(a, b):
    return lax.dot_general(a, b, (((0,), (0,)), ((), ())), preferred_element_type=F32)


def _bf(x):
    return x.astype(BF16)


def _split_terms(x, n):
    terms = []
    r = x
    for t in range(n):
        p = r.astype(BF16)
        terms.append(p)
        if t + 1 < n:
            r = r - p.astype(F32)
    return terms


def _mask_dot(m01, x, n=3):
    acc = None
    for p in _split_terms(x, n):
        t = _dot(m01, p)
        acc = t if acc is None else acc + t
    return acc


def _dot_mask(x, m01, n=3):
    acc = None
    for p in _split_terms(x, n):
        t = _dot(p, m01)
        acc = t if acc is None else acc + t
    return acc


def _dot3(a, b):
    ah, al = _split_terms(a, 2)
    bh, bl = _split_terms(b, 2)
    return _dot(ah, bh) + (_dot(ah, bl) + _dot(al, bh))


def _scan_masks(direction):
    ri = lax.broadcasted_iota(jnp.int32, (CHUNK, CHUNK), 0)
    ci = lax.broadcasted_iota(jnp.int32, (CHUNK, CHUNK), 1)
    if direction == 0:
        return ci <= ri, ci < ri
    return ci >= ri, ci > ri


def _unit_tri_inverse(a):
    ri = lax.broadcasted_iota(jnp.int32, (CHUNK, CHUNK), 0)
    ci = lax.broadcasted_iota(jnp.int32, (CHUNK, CHUNK), 1)
    eye = jnp.where(ri == ci, 1.0, 0.0).astype(F32)
    s1 = _dot3(a, a)
    s2 = _dot3(s1, s1)
    s3 = _dot3(s2, s2)
    s4 = _dot3(s3, s3)
    s5 = _dot3(s4, s4)
    p1 = _dot3(eye - a, eye + s1)
    p2 = _dot3(eye + s2, eye + s3)
    p3 = _dot3(eye + s4, eye + s5)
    return _dot3(_dot3(p1, p2), p3)


def _conv_window(win, w_ref, cols, bias=None):
    acc = None
    for j in range(CONV_K):
        lo = HALO - CONV_K // 2 + j
        t = w_ref[j:j + 1, cols] * win[lo:lo + CHUNK]
        acc = t if acc is None else acc + t
    if bias is not None:
        acc = acc + bias
    return acc


def _conv_rows(xp_ref, d, row0, w_ref, cols):
    acc = None
    for j in range(CONV_K):
        lo = row0 + HALO - CONV_K // 2 + j
        t = w_ref[j:j + 1, cols] * xp_ref[d, lo:lo + CHUNK, cols]
        acc = t if acc is None else acc + t
    return acc


def _stage_halo(xp_ref, d, main, prev, nxt, blk, nblk, rows):
    xp_ref[d, 0:HALO, :] = jnp.where(blk == 0, 0.0, prev[0])
    xp_ref[d, HALO:HALO + rows, :] = main[0]
    xp_ref[d, HALO + rows:rows + 2 * HALO, :] = jnp.where(blk == nblk - 1, 0.0, nxt[0])


def _ffn_kernel(x_ref, g_ref, wg_ref, wu_ref, wd_ref, *rest, final_norm):
    if final_norm:
        fg_ref, o_ref, acc_ref = rest
    else:
        o_ref, acc_ref = rest
    x = x_ref[...]
    h = _bf(_rms(x, g_ref[...]))
    for c in range(D_FF // FF_TILE):
        sl = slice(c * FF_TILE, (c + 1) * FF_TILE)
        gate = _dot(h, wg_ref[:, sl])
        up = _dot(h, wu_ref[:, sl])
        act = _bf(_silu(gate) * up)
        part = _dot(act, wd_ref[sl, :])
        if c == 0:
            acc_ref[...] = part
        else:
            acc_ref[...] += part
    y = x + 0.5 * acc_ref[...]
    if final_norm:
        y = _rms(y, fg_ref[...])
    o_ref[...] = y


def _resident(shape):
    return pl.BlockSpec(shape, lambda *_: (0,) * len(shape), pipeline_mode=pl.Buffered(1))


def _ffn(x2d, norm_g, wg, wu, wd, final_g=None):
    t = x2d.shape[0]
    tm = min(TM_DENSE, t)
    row = pl.BlockSpec((tm, D_MODEL), lambda i: (i, 0))
    in_specs = [row, _resident((1, D_MODEL)), _resident((D_MODEL, D_FF)),
                _resident((D_MODEL, D_FF)), _resident((D_FF, D_MODEL))]
    args = [x2d, norm_g.reshape(1, D_MODEL), wg, wu, wd]
    if final_g is not None:
        in_specs.append(_resident((1, D_MODEL)))
        args.append(final_g.reshape(1, D_MODEL))
    return pl.pallas_call(
        functools.partial(_ffn_kernel, final_norm=final_g is not None),
        out_shape=jax.ShapeDtypeStruct((t, D_MODEL), F32),
        grid=(t // tm,),
        in_specs=in_specs,
        out_specs=row,
        scratch_shapes=[pltpu.VMEM((tm, D_MODEL), F32)],
        compiler_params=pltpu.CompilerParams(
            dimension_semantics=("arbitrary",), vmem_limit_bytes=VMEM_LIMIT),
        name="ffn_final" if final_g is not None else "ffn",
    )(*args)


def _inproj_kernel(x_ref, g_ref, w_ref, o_ref):
    h = _bf(_rms(x_ref[...], g_ref[...]))
    o_ref[...] = _dot(h, w_ref[...])


def _inproj(x2d, norm_g, w):
    t = x2d.shape[0]
    n = w.shape[1]
    tm = min(TM_DENSE, t)
    return pl.pallas_call(
        _inproj_kernel,
        out_shape=jax.ShapeDtypeStruct((t, n), F32),
        grid=(t // tm,),
        in_specs=[pl.BlockSpec((tm, D_MODEL), lambda i: (i, 0)),
                  _resident((1, D_MODEL)), _resident((D_MODEL, n))],
        out_specs=pl.BlockSpec((tm, n), lambda i: (i, 0)),
        compiler_params=pltpu.CompilerParams(
            dimension_semantics=("arbitrary",), vmem_limit_bytes=VMEM_LIMIT),
        name="inproj",
    )(x2d, norm_g.reshape(1, D_MODEL), w)


def _head_rms(o, g_row, width):
    parts = []
    for s in range(0, o.shape[-1], width):
        blk = o[:, s:s + width]
        ms = jnp.mean(blk * blk, axis=-1, keepdims=True)
        parts.append(blk * lax.rsqrt(ms + NORM_EPS))
    return jnp.concatenate(parts, axis=-1) * g_row


def _merge_kernel(x_ref, oaf_ref, oab_ref, za_ref, obf_ref, obb_ref, rb_ref,
                  ocf_ref, ocb_ref, zc_ref, gate_ref, na_ref, nb_ref, nc_ref,
                  wbr_ref, wout_ref, o_ref):
    ya = _head_rms(oaf_ref[...] + oab_ref[...], na_ref[...], GDN_DV) * _silu(za_ref[...])
    yb = _head_rms(obf_ref[...] + obb_ref[...], nb_ref[...], GLA_DV) * _silu(rb_ref[...])
    yc = (ocf_ref[...] + ocb_ref[...]) * _silu(zc_ref[...])
    yc = _head_rms(yc, nc_ref[...], SSD_INNER // SSD_GROUPS)
    merged = None
    for b, y in enumerate((ya, yb, yc)):
        gate = jax.nn.sigmoid(gate_ref[:, b * D_MODEL:(b + 1) * D_MODEL])
        t = gate * _dot(_bf(y), wbr_ref[b])
        merged = t if merged is None else merged + t
    o_ref[...] = x_ref[...] + _dot(_bf(merged), wout_ref[...])


def _merge(x2d, oa, za_slab, ob, zb_slab, oc, zc_slab, gates, na, nb, nc, wbr, wout):
    t = x2d.shape[0]
    tm = min(TM_MERGE, t)

    def col(width, blk):
        return pl.BlockSpec((tm, width), lambda i: (i, blk))

    in_specs = [
        col(D_MODEL, 0),
        col(MIX_W, 0), col(MIX_W, 0), col(MIX_W, GDN_CONV_CH // MIX_W),
        col(MIX_W, 0), col(MIX_W, 0), col(MIX_W, (2 * GLA_KW + GLA_VW) // MIX_W),
        col(MIX_W, 0), col(MIX_W, 0), col(MIX_W, SSD_CONV_CH // MIX_W),
        col(GATE_SLAB, 0),
        _resident((1, MIX_W)), _resident((1, MIX_W)), _resident((1, MIX_W)),
        _resident((N_BRANCH, MIX_W, D_MODEL)), _resident((D_MODEL, D_MODEL)),
    ]
    return pl.pallas_call(
        _merge_kernel,
        out_shape=jax.ShapeDtypeStruct((t, D_MODEL), F32),
        grid=(t // tm,),
        in_specs=in_specs,
        out_specs=col(D_MODEL, 0),
        compiler_params=pltpu.CompilerParams(
            dimension_semantics=("arbitrary",), vmem_limit_bytes=VMEM_LIMIT),
        name="merge",
    )(x2d, oa[0], oa[1], za_slab, ob[0], ob[1], zb_slab, oc[0], oc[1], zc_slab, gates,
      na, nb, nc, wbr, wout)


GDN_PW = GDN_HEADS * CHUNK


def _lane_groups(cols, width):
    per_tile = LANES // width
    tiles = []
    for t0 in range(0, len(cols), per_tile):
        lane = lax.broadcasted_iota(jnp.int32, (CHUNK, LANES), 1)
        tile = jnp.broadcast_to(cols[t0], (CHUNK, LANES))
        for s in range(1, per_tile):
            tile = jnp.where(lane >= s * width, jnp.broadcast_to(cols[t0 + s], (CHUNK, LANES)), tile)
        tiles.append(tile)
    return jnp.concatenate(tiles, axis=-1) if len(tiles) > 1 else tiles[0]


def _block_diag(y, mask01):
    return jnp.concatenate([y] * (mask01.shape[0] // y.shape[0]), axis=0) * mask01


def _packed_dot3(pairs, bd_ref):
    split = {}

    def terms(a):
        if id(a) not in split:
            split[id(a)] = _split_terms(a, 2)
        return split[id(a)]

    ops = []
    for x, y in pairs:
        xh, xl = terms(x)
        yh, yl = terms(y)
        ops.append((xh, xl, _block_diag(yh, bd_ref[...]), _block_diag(yl, bd_ref[...])))
    return [_dot(xh, bh) + (_dot(xh, bl) + _dot(xl, bh)) for xh, xl, bh, bl in ops]


def _gdn_group_setup(d, c, xp_ref, small, convw_ref, prow_ref, bdk_ref,
                     r_ref, wq_ref, att_ref, kg_ref, gl_ref):
    base = c * CHUNK
    ri = lax.broadcasted_iota(jnp.int32, (CHUNK, GDN_PW), 0)
    ci = lax.broadcasted_iota(jnp.int32, (CHUNK, GDN_PW), 1) & (CHUNK - 1)
    incl_p = (ci <= ri) if d == 0 else (ci >= ri)
    strict_p = (ci < ri) if d == 0 else (ci > ri)
    incl, _ = _scan_masks(d)
    m_incl = jnp.where(incl, 1.0, 0.0).astype(BF16)

    sm = small[0, base:base + CHUNK, :]
    neg_a = prow_ref[2 * d:2 * d + 1, :]
    dtb = prow_ref[2 * d + 1:2 * d + 2, :]
    beta_all = jax.nn.sigmoid(sm)
    g_all = neg_a * _softplus(sm + dtb)
    gcum = _mask_dot(m_incl, g_all)
    gcum_t = gcum.T
    last = CHUNK - 1 if d == 0 else 0
    heads = range(GDN_HEADS)
    g_cols = [gcum[:, GDN_HEADS + h:GDN_HEADS + h + 1] for h in heads]
    g_col_p = _lane_groups(g_cols, CHUNK)
    g_row_p = jnp.concatenate(
        [gcum_t[GDN_HEADS + h:GDN_HEADS + h + 1, :] for h in heads], axis=-1)
    diff = g_col_p - g_row_p
    dec_strict = jnp.exp(jnp.where(strict_p, diff, NEG_BIG))
    dec_incl = jnp.exp(jnp.where(incl_p, diff, NEG_BIG))

    def conv(off):
        outs = []
        for h in heads:
            cols = slice(off + h * LANES, off + (h + 1) * LANES)
            outs.append(_silu(_conv_rows(xp_ref, d, base, convw_ref, cols)))
        return outs

    def l2n(t):
        return t * lax.rsqrt(jnp.sum(t * t, axis=-1, keepdims=True) + NORM_EPS)

    q = jnp.concatenate([l2n(t) * (GDN_DK ** -0.5) for t in conv(0)], axis=-1)
    k = jnp.concatenate([l2n(t) for t in conv(GDN_KW)], axis=-1)
    v = jnp.concatenate(conv(2 * GDN_KW), axis=-1)
    beta_x = _lane_groups([beta_all[:, h:h + 1] for h in heads], LANES)
    g_x = _lane_groups(g_cols, LANES)
    g_last_x = g_x[last:last + 1, :]
    e_g = jnp.exp(g_x)
    kb = k * beta_x
    k_bd = _block_diag(_bf(k), bdk_ref[...])
    a_mat = _dot_nt(_bf(kb), k_bd) * dec_strict
    att = _dot_nt(_bf(q), k_bd) * dec_incl
    r_ref[d, c, :, 0:GDN_KW] = kb * e_g
    r_ref[d, c, :, GDN_KW:GDN_KW + GDN_VW] = v * beta_x
    qg = _bf(q * e_g)
    kg = _bf(k * jnp.exp(g_last_x - g_x))
    e_last = jnp.exp(g_last_x)
    for h in heads:
        hs = slice(h * LANES, (h + 1) * LANES)
        wq_ref[d, c, h, CHUNK:2 * CHUNK, :] = qg[:, hs]
        att_ref[d, c, h] = _bf(att[:, h * CHUNK:(h + 1) * CHUNK])
        kg_ref[d, c, h] = kg[:, hs]
        gl_ref[d, c, h] = jnp.broadcast_to(e_last[:, hs], (SUBLANES, LANES))
    return a_mat


def _gdn_prep(nc, xp_ref, smalls, convw_ref, prow_ref, bda_ref, bdk_ref, bdr_ref,
              r_ref, wq_ref, u_ref, att_ref, kg_ref, gl_ref):
    groups = [(d, c) for c in range(nc) for d in range(2)]
    a = [_gdn_group_setup(d, c, xp_ref, smalls[d], convw_ref, prow_ref, bdk_ref,
                          r_ref, wq_ref, att_ref, kg_ref, gl_ref) for d, c in groups]
    n = len(groups)
    ri = lax.broadcasted_iota(jnp.int32, (CHUNK, GDN_PW), 0)
    ci = lax.broadcasted_iota(jnp.int32, (CHUNK, GDN_PW), 1) & (CHUNK - 1)
    eye = jnp.where(ri == ci, 1.0, 0.0).astype(F32)
    s1 = _packed_dot3([(x, x) for x in a], bda_ref)
    lvl = _packed_dot3([(x, x) for x in s1] + [(eye - x, eye + y) for x, y in zip(a, s1)], bda_ref)
    s2, p1 = lvl[:n], lvl[n:]
    s3 = _packed_dot3([(x, x) for x in s2], bda_ref)
    lvl = _packed_dot3([(x, x) for x in s3] + [(eye + x, eye + y) for x, y in zip(s2, s3)], bda_ref)
    s4, p2 = lvl[:n], lvl[n:]
    lvl = _packed_dot3([(x, x) for x in s4] + list(zip(p1, p2)), bda_ref)
    s5, p12 = lvl[:n], lvl[n:]
    p3 = _packed_dot3([(eye + x, eye + y) for x, y in zip(s4, s5)], bda_ref)
    t_inv = _packed_dot3(list(zip(p12, p3)), bda_ref)
    sol = _packed_dot3([(t, r_ref[d, c]) for t, (d, c) in zip(t_inv, groups)], bdr_ref)
    for (d, c), wu in zip(groups, sol):
        for h in range(GDN_HEADS):
            wq_ref[d, c, h, 0:CHUNK, :] = _bf(wu[:, h * LANES:(h + 1) * LANES])
            u_ref[d, c, h] = wu[:, GDN_KW + h * LANES:GDN_KW + (h + 1) * LANES]


def _gdn_scan_step(units, wq_ref, u_ref, att_ref, kg_ref, gl_ref, s_ref):
    states = [s_ref[d, h] for d, _, h, _ in units]
    r = [_dot(wq_ref[d, c, h], _bf(s)) for (d, c, h, _), s in zip(units, states)]
    vb = [_bf(u_ref[d, c, h] - x[0:CHUNK]) for (d, c, h, _), x in zip(units, r)]
    for (d, c, h, o_ref), s, x, v in zip(units, states, r, vb):
        base = pl.multiple_of(c * CHUNK, CHUNK)
        o_ref[0, pl.ds(base, CHUNK), h * LANES:(h + 1) * LANES] = (
            x[CHUNK:2 * CHUNK] + _dot(att_ref[d, c, h], v))
        s_ref[d, h] = s * gl_ref[d, c, h, 0:1, :] + _dot_tn(kg_ref[d, c, h], v)


def _gdn_kernel(main_f, prev_f, next_f, small_f, main_b, prev_b, next_b, small_b,
                convw_ref, prow_ref, bda_ref, bdk_ref, bdr_ref, of_ref, ob_ref,
                xp_ref, r_ref, wq_ref, u_ref, att_ref, kg_ref, gl_ref, s_ref, *, rows, nblk):
    i = pl.program_id(1)
    nc = rows // CHUNK

    @pl.when(i == 0)
    def _():
        s_ref[...] = jnp.zeros_like(s_ref)

    _stage_halo(xp_ref, 0, main_f, prev_f, next_f, i, nblk, rows)
    _stage_halo(xp_ref, 1, main_b, prev_b, next_b, nblk - 1 - i, nblk, rows)
    _gdn_prep(nc, xp_ref, (small_f, small_b), convw_ref, prow_ref, bda_ref, bdk_ref, bdr_ref,
              r_ref, wq_ref, u_ref, att_ref, kg_ref, gl_ref)

    def scan(j, carry):
        units = [(0, j, h, of_ref) for h in range(GDN_HEADS)]
        units += [(1, nc - 1 - j, h, ob_ref) for h in range(GDN_HEADS)]
        _gdn_scan_step(units, wq_ref, u_ref, att_ref, kg_ref, gl_ref, s_ref)
        return carry

    lax.fori_loop(0, nc, scan, 0, unroll=LOOP_UNROLL)


def _bidir_specs(rows, nblk, width, col_blk, halo):
    rb = rows // HALO
    last = nblk * rb - 1

    def mk(blk_of):
        specs = [pl.BlockSpec((1, rows, width), lambda b, i: (b, blk_of(i), col_blk))]
        if halo:
            specs.append(pl.BlockSpec(
                (1, HALO, width), lambda b, i: (b, jnp.maximum(blk_of(i) * rb - 1, 0), col_blk)))
            specs.append(pl.BlockSpec(
                (1, HALO, width), lambda b, i: (b, jnp.minimum((blk_of(i) + 1) * rb, last), col_blk)))
        return specs

    return mk(lambda i: i), mk(lambda i: nblk - 1 - i)


def _mixer_out(bsz, seq, rows, nblk):
    shape = jax.ShapeDtypeStruct((bsz, seq, MIX_W), F32)
    specs = [pl.BlockSpec((1, rows, MIX_W), lambda b, i: (b, i, 0)),
             pl.BlockSpec((1, rows, MIX_W), lambda b, i: (b, nblk - 1 - i, 0))]
    return [shape, shape], specs


def _gdn_mixer(slab, conv_w, a_log, dt_bias):
    bsz, seq, _ = slab.shape
    rows = min(MIX_ROWS, seq)
    nblk = seq // rows
    nc = rows // CHUNK
    f_main, b_main = _bidir_specs(rows, nblk, GDN_CONV_CH, 0, True)
    small_blk = (GDN_CONV_CH + GDN_VW) // LANES
    f_small, _ = _bidir_specs(rows, nblk, LANES, small_blk, False)
    _, b_small = _bidir_specs(rows, nblk, LANES, small_blk + 1, False)
    convw = jnp.zeros((SUBLANES, GDN_CONV_CH), F32).at[:CONV_K].set(conv_w)
    prow = jnp.zeros((SUBLANES, LANES), F32)
    for d in range(2):
        prow = prow.at[2 * d, GDN_HEADS:2 * GDN_HEADS].set(-jnp.exp(a_log[d]))
        prow = prow.at[2 * d + 1, GDN_HEADS:2 * GDN_HEADS].set(dt_bias[d])
    out_shape, out_specs = _mixer_out(bsz, seq, rows, nblk)
    hd = (2, nc, GDN_HEADS)
    row_head = np.arange(GDN_PW)[:, None] // CHUNK
    bda = jnp.asarray(row_head == np.arange(GDN_PW)[None, :] // CHUNK, BF16)
    bdk = jnp.asarray(row_head == np.arange(GDN_KW)[None, :] // GDN_DK, BF16)
    bdr = jnp.asarray(row_head == (np.arange(GDN_KW + GDN_VW)[None, :] % GDN_KW) // GDN_DK, BF16)
    return pl.pallas_call(
        functools.partial(_gdn_kernel, rows=rows, nblk=nblk),
        out_shape=out_shape,
        grid=(bsz, nblk),
        in_specs=f_main + f_small + b_main + b_small + [
            _resident((SUBLANES, GDN_CONV_CH)), _resident((SUBLANES, LANES)),
            _resident(bda.shape), _resident(bdk.shape), _resident(bdr.shape)],
        out_specs=out_specs,
        scratch_shapes=[
            pltpu.VMEM((2, rows + 2 * HALO, GDN_CONV_CH), F32),
            pltpu.VMEM((2, nc, CHUNK, GDN_KW + GDN_VW), F32),
            pltpu.VMEM(hd + (2 * CHUNK, GDN_DK), BF16),
            pltpu.VMEM(hd + (CHUNK, GDN_DV), F32),
            pltpu.VMEM(hd + (CHUNK, CHUNK), BF16),
            pltpu.VMEM(hd + (CHUNK, GDN_DK), BF16),
            pltpu.VMEM(hd + (SUBLANES, LANES), F32),
            pltpu.VMEM((2, GDN_HEADS, GDN_DK, GDN_DV), F32),
        ],
        compiler_params=pltpu.CompilerParams(
            dimension_semantics=("arbitrary", "arbitrary"), vmem_limit_bytes=VMEM_LIMIT),
        name="gdn",
    )(slab, slab, slab, slab, slab, slab, slab, slab, convw, prow, bda, bdk, bdr)


GLA_LEVELS = 6


def _gla_tables(direction):
    pos = np.arange(CHUNK)
    order = pos if direction == 0 else CHUNK - 1 - pos
    wst = np.zeros((GLA_LEVELS + 1, CHUNK, CHUNK), np.float32)
    msk = np.zeros((GLA_LEVELS + 1, CHUNK, CHUNK), np.float32)
    for lvl in range(GLA_LEVELS):
        s = CHUNK >> (lvl + 1)
        blk = order // (2 * s)
        later = (order % (2 * s)) >= s
        bound = blk * 2 * s + s - 1
        for i in range(CHUNK):
            if later[i]:
                wst[lvl, i] = (order > bound[i]) & (order <= order[i])
            else:
                wst[lvl, i] = (order > order[i]) & (order <= bound[i])
        msk[lvl] = (blk[:, None] == blk[None, :]) & later[:, None] & (~later[None, :])
    wst[GLA_LEVELS] = order[None, :] <= order[:, None]
    msk[GLA_LEVELS] = np.eye(CHUNK)
    return wst.reshape((GLA_LEVELS + 1) * CHUNK, CHUNK), msk


def _gla_chunks(groups, wst_ref, msk_ref, bdk_ref, bdv_ref, wg_ref, bg_ref, s_ref):
    n = len(groups)
    rows = [pl.ds(pl.multiple_of(c * CHUNK, CHUNK), CHUNK) for _, c, _, _, _ in groups]
    q = [qkv[0, r, 0:GLA_KW] * (GLA_DK ** -0.5) for (_, _, qkv, _, _), r in zip(groups, rows)]
    k = [qkv[0, r, GLA_KW:2 * GLA_KW] for (_, _, qkv, _, _), r in zip(groups, rows)]
    pre = [_dot(_bf(lr_ref[0, r, :]), wg_ref[d]) + bg_ref[d:d + 1, :]
           for (d, _, _, lr_ref, _), r in zip(groups, rows)]
    gk = [-_softplus(-p) * (1.0 / GLA_TAU) for p in pre]
    e_all = [_mask_dot(wst_ref[d], g, n=2) for (d, _, _, _, _), g in zip(groups, gk)]
    att = [None] * n
    for lvl in range(GLA_LEVELS + 1):
        for i, (d, _, _, _, _) in enumerate(groups):
            if lvl < GLA_LEVELS:
                x = jnp.exp(e_all[i][lvl * CHUNK:(lvl + 1) * CHUNK])
                qx, kx = _bf(q[i] * x), _bf(k[i] * x)
            else:
                qx, kx = _bf(q[i]), _bf(k[i])
            p = msk_ref[d, lvl] * _dot_nt(qx, _block_diag(kx, bdk_ref[...]))
            att[i] = p if att[i] is None else att[i] + p
    for i, (d, _, qkv, _, o_ref) in enumerate(groups):
        gcum = e_all[i][GLA_LEVELS * CHUNK:(GLA_LEVELS + 1) * CHUNK]
        last = CHUNK - 1 if d == 0 else 0
        g_last = gcum[last:last + 1, :]
        qg = _bf(q[i] * jnp.exp(gcum))
        kg = _bf(k[i] * jnp.exp(g_last - gcum))
        e_last = jnp.exp(g_last)
        v = _bf(qkv[0, rows[i], 2 * GLA_KW:2 * GLA_KW + GLA_VW])
        o_intra = _dot(_bf(att[i]), _block_diag(v, bdv_ref[...]))
        for h in range(GLA_HEADS):
            hs = slice(h * GLA_DK, (h + 1) * GLA_DK)
            vs = slice(h * GLA_DV, (h + 1) * GLA_DV)
            st = s_ref[d, h]
            o_ref[0, rows[i], vs] = o_intra[:, vs] + _dot_nt(qg[:, hs], _bf(st))
            s_ref[d, h] = st * e_last[:, hs] + _dot_tn(v[:, vs], kg[:, hs])


def _gla_kernel(qkv_f, lr_f, qkv_b, lr_b, wst_ref, msk_ref, bdk_ref, bdv_ref, wg_ref, bg_ref,
                of_ref, ob_ref, s_ref, *, rows):
    i = pl.program_id(1)
    nc = rows // CHUNK

    @pl.when(i == 0)
    def _():
        s_ref[...] = jnp.zeros_like(s_ref)

    def body(j, carry):
        groups = []
        for t in range(2):
            groups += [(0, 2 * j + t, qkv_f, lr_f, of_ref), (1, nc - 1 - 2 * j - t, qkv_b, lr_b, ob_ref)]
        _gla_chunks(groups, wst_ref, msk_ref, bdk_ref, bdv_ref, wg_ref, bg_ref, s_ref)
        return carry

    lax.fori_loop(0, nc // 2, body, 0)


def _gla_mixer(slab, w_gup, b_g):
    bsz, seq, _ = slab.shape
    rows = min(MIX_ROWS, seq)
    nblk = seq // rows
    assert rows % (2 * CHUNK) == 0
    qkv_w = 2 * GLA_KW + GLA_VW
    f_qkv, b_qkv = _bidir_specs(rows, nblk, qkv_w, 0, False)
    lr_blk = (2 * GLA_KW + 2 * GLA_VW) // LANES
    f_lr, b_lr = _bidir_specs(rows, nblk, LANES, lr_blk, False)
    tabs = [_gla_tables(d) for d in range(2)]
    wst = jnp.asarray(np.stack([t[0] for t in tabs]), BF16)
    msk = jnp.asarray(np.tile(np.stack([t[1] for t in tabs]), (1, 1, 1, GLA_HEADS)), F32)
    row_head = np.arange(GLA_HEADS * CHUNK)[:, None] // CHUNK
    bdk = jnp.asarray(row_head == np.arange(GLA_KW)[None, :] // GLA_DK, BF16)
    bdv = jnp.asarray(row_head == np.arange(GLA_VW)[None, :] // GLA_DV, BF16)
    wg = jnp.zeros((2, LANES, GLA_KW), F32)
    for d in range(2):
        wg = wg.at[d, d * GLA_RANK:(d + 1) * GLA_RANK].set(w_gup[d])
    out_shape, out_specs = _mixer_out(bsz, seq, rows, nblk)
    return pl.pallas_call(
        functools.partial(_gla_kernel, rows=rows),
        out_shape=out_shape,
        grid=(bsz, nblk),
        in_specs=f_qkv + f_lr + b_qkv + b_lr + [
            _resident(wst.shape), _resident(msk.shape), _resident(bdk.shape), _resident(bdv.shape),
            _resident((2, LANES, GLA_KW)), _resident((2, GLA_KW))],
        out_specs=out_specs,
        scratch_shapes=[pltpu.VMEM((2, GLA_HEADS, GLA_DV, GLA_DK), F32)],
        compiler_params=pltpu.CompilerParams(
            dimension_semantics=("arbitrary", "arbitrary"), vmem_limit_bytes=VMEM_LIMIT),
        name="gla",
    )(slab, slab, slab, slab, wst, msk, bdk, bdv, wg.astype(BF16), b_g.astype(F32))


def _ssd_chunk(d, c, xp_ref, small, convw_ref, convb_ref, prow_ref, expand_ref, dskip_ref,
               s_ref, o_ref):
    incl, _ = _scan_masks(d)
    m_incl = jnp.where(incl, 1.0, 0.0).astype(BF16)
    base = pl.multiple_of(c * CHUNK, CHUNK)
    rows = pl.ds(base, CHUNK)

    def conv(lo, hi):
        cols = slice(lo, hi)
        win = xp_ref[d, pl.ds(base, CHUNK + 2 * HALO), cols]
        return _silu(_conv_window(win, convw_ref, cols, convb_ref[:, cols]))

    sm = small[0, rows, :]
    neg_a = prow_ref[2 * d:2 * d + 1, :]
    dtb = prow_ref[2 * d + 1:2 * d + 2, :]
    dt = _softplus(sm + dtb)
    acum = _mask_dot(m_incl, dt * neg_a)
    acum_t = acum.T
    expand = expand_ref[...]
    dt_x = _dot_mask(dt, expand)
    acum_x = _dot_mask(acum, expand)
    last = CHUNK - 1 if d == 0 else 0
    a_last = acum_x[last:last + 1, :]
    e_acum = jnp.exp(acum_x)
    e_tail = jnp.exp(a_last - acum_x)
    e_last = jnp.exp(a_last)
    gw = SSD_HPG * SSD_HEADDIM
    for g in range(SSD_GROUPS):
        gs = slice(g * gw, (g + 1) * gw)
        x = conv(g * gw, (g + 1) * gw)
        bm = conv(SSD_INNER + g * SSD_STATE, SSD_INNER + (g + 1) * SSD_STATE)
        cm = conv(SSD_INNER + SSD_BCW + g * SSD_STATE, SSD_INNER + SSD_BCW + (g + 1) * SSD_STATE)
        xdt = x * dt_x[:, gs]
        xdt_b = _bf(xdt)
        cb = _dot_nt(_bf(cm), _bf(bm))
        st = s_ref[d, g]
        y = _dot(_bf(cm), _bf(st)) * e_acum[:, gs]
        parts = []
        for hh in range(SSD_HPG):
            hd = g * SSD_HPG + hh
            diff = acum[:, hd:hd + 1] - acum_t[hd:hd + 1, :]
            lmat = jnp.exp(jnp.where(incl, diff, NEG_BIG))
            ps = slice(hh * SSD_HEADDIM, (hh + 1) * SSD_HEADDIM)
            parts.append(_dot(_bf(cb * lmat), xdt_b[:, ps]))
        y = y + jnp.concatenate(parts, axis=-1)
        if d == 0:
            y = y + x * dskip_ref[:, gs]
        s_ref[d, g] = st * e_last[:, gs] + _dot_tn(_bf(bm), _bf(xdt * e_tail[:, gs]))
        o_ref[0, rows, gs] = y


def _ssd_kernel(main_f, prev_f, next_f, small_f, main_b, prev_b, next_b, small_b,
                convw_ref, convb_ref, prow_ref, expand_ref, dskip_ref, of_ref, ob_ref,
                xp_ref, s_ref, *, rows, nblk):
    i = pl.program_id(1)
    nc = rows // CHUNK

    @pl.when(i == 0)
    def _():
        s_ref[...] = jnp.zeros_like(s_ref)

    _stage_halo(xp_ref, 0, main_f, prev_f, next_f, i, nblk, rows)
    _stage_halo(xp_ref, 1, main_b, prev_b, next_b, nblk - 1 - i, nblk, rows)

    def body(j, carry):
        _ssd_chunk(0, j, xp_ref, small_f, convw_ref, convb_ref, prow_ref, expand_ref, dskip_ref,
                   s_ref, of_ref)
        _ssd_chunk(1, nc - 1 - j, xp_ref, small_b, convw_ref, convb_ref, prow_ref, expand_ref,
                   dskip_ref, s_ref, ob_ref)
        return carry

    lax.fori_loop(0, nc, body, 0, unroll=LOOP_UNROLL)


def _ssd_mixer(slab, conv_w, conv_b, a_log, dt_bias, d_skip):
    bsz, seq, _ = slab.shape
    rows = min(MIX_ROWS, seq)
    nblk = seq // rows
    f_main, b_main = _bidir_specs(rows, nblk, SSD_CONV_CH, 0, True)
    small_blk = (SSD_CONV_CH + SSD_INNER) // LANES
    f_small, _ = _bidir_specs(rows, nblk, LANES, small_blk, False)
    _, b_small = _bidir_specs(rows, nblk, LANES, small_blk + 1, False)
    convw = jnp.zeros((SUBLANES, SSD_CONV_CH), F32).at[:CONV_K].set(conv_w)
    convb = conv_b.reshape(1, SSD_CONV_CH).astype(F32)
    prow = jnp.zeros((SUBLANES, LANES), F32)
    for d in range(2):
        prow = prow.at[2 * d, 0:SSD_HEADS].set(-jnp.exp(a_log[d]))
        prow = prow.at[2 * d + 1, 0:SSD_HEADS].set(dt_bias[d])
    expand = np.zeros((LANES, SSD_INNER), np.float32)
    for hd in range(SSD_HEADS):
        expand[hd, hd * SSD_HEADDIM:(hd + 1) * SSD_HEADDIM] = 1.0
    dskip = jnp.repeat(d_skip.astype(F32), SSD_HEADDIM).reshape(1, SSD_INNER)
    out_shape, out_specs = _mixer_out(bsz, seq, rows, nblk)
    return pl.pallas_call(
        functools.partial(_ssd_kernel, rows=rows, nblk=nblk),
        out_shape=out_shape,
        grid=(bsz, nblk),
        in_specs=f_main + f_small + b_main + b_small + [
            _resident((SUBLANES, SSD_CONV_CH)), _resident((1, SSD_CONV_CH)),
            _resident((SUBLANES, LANES)), _resident((LANES, SSD_INNER)),
            _resident((1, SSD_INNER))],
        out_specs=out_specs,
        scratch_shapes=[
            pltpu.VMEM((2, rows + 2 * HALO, SSD_CONV_CH), F32),
            pltpu.VMEM((2, SSD_GROUPS, SSD_STATE, SSD_HPG * SSD_HEADDIM), F32),
        ],
        compiler_params=pltpu.CompilerParams(
            dimension_semantics=("arbitrary", "arbitrary"), vmem_limit_bytes=VMEM_LIMIT),
        name="ssd",
    )(slab, slab, slab, slab, slab, slab, slab, slab, convw, convb, prow,
      jnp.asarray(expand, BF16), dskip)


def _slab_columns():
    o = dict(zip(("a_qkv", "a_z", "a_b", "a_a", "b_q", "b_k", "b_v", "b_r", "b_g",
                  "c_z", "c_xbc", "c_dt", "gate"), IN_OFFS))
    pad = IN_COLS

    def rng(start, n):
        return list(range(start, start + n))

    def padded(cols):
        return cols + [pad] * (LANES - len(cols))

    gdn = rng(o["a_qkv"], GDN_CONV_CH) + rng(o["a_z"], GDN_VW)
    for d in range(2):
        gdn += padded(rng(o["a_b"] + d * GDN_HEADS, GDN_HEADS) + rng(o["a_a"] + d * GDN_HEADS, GDN_HEADS))
    gla = (rng(o["b_q"], GLA_KW) + rng(o["b_k"], GLA_KW) + rng(o["b_v"], GLA_VW)
           + rng(o["b_r"], GLA_VW) + padded(rng(o["b_g"], 2 * GLA_RANK)))
    ssd = rng(o["c_xbc"], SSD_CONV_CH) + rng(o["c_z"], SSD_INNER)
    for d in range(2):
        ssd += padded(rng(o["c_dt"] + d * SSD_HEADS, SSD_HEADS))
    gate = rng(o["gate"], GATE_SLAB)
    assert len(gdn) == GDN_SLAB and len(gla) == GLA_SLAB and len(ssd) == SSD_SLAB
    return tuple(np.asarray(c, np.int32) for c in (gdn, gla, ssd, gate))


_SLAB_COLS = _slab_columns()


def _layer(x, p, final_g):
    bsz, seq, _ = x.shape
    t = bsz * seq
    x2d = x.reshape(t, D_MODEL)
    x2d = _ffn(x2d, p["ffn_norm"][0], _bf(p["ffn_w_gate"][0]), _bf(p["ffn_w_up"][0]),
               _bf(p["ffn_w_down"][0]))
    w_ext = jnp.concatenate([p["w_in"], jnp.zeros((D_MODEL, 1), F32)], axis=1)
    slabs = [_inproj(x2d, p["mix_norm"], _bf(jnp.take(w_ext, cols, axis=1))) for cols in _SLAB_COLS]
    za, zb, zc, gates = slabs
    oa = _gdn_mixer(za.reshape(bsz, seq, GDN_SLAB), p["gdn_conv_w"], p["gdn_a_log"], p["gdn_dt_bias"])
    ob = _gla_mixer(zb.reshape(bsz, seq, GLA_SLAB), p["gla_w_gup"], p["gla_b_g"])
    oc = _ssd_mixer(zc.reshape(bsz, seq, SSD_SLAB), p["ssd_conv_w"], p["ssd_conv_b"],
                    p["ssd_a_log"], p["ssd_dt_bias"], p["ssd_d"])
    flat = lambda pair: [o.reshape(t, MIX_W) for o in pair]
    na = jnp.tile(p["gdn_norm"], GDN_HEADS).reshape(1, MIX_W)
    nb = jnp.tile(p["gla_norm"], GLA_HEADS).reshape(1, MIX_W)
    nc = p["ssd_norm"].reshape(1, MIX_W)
    x2d = _merge(x2d, flat(oa), za, flat(ob), zb, flat(oc), zc, gates, na, nb, nc,
                 _bf(p["w_branch"]), _bf(p["w_out"]))
    x2d = _ffn(x2d, p["ffn_norm"][1], _bf(p["ffn_w_gate"][1]), _bf(p["ffn_w_up"][1]),
               _bf(p["ffn_w_down"][1]), final_g)
    return x2d.reshape(bsz, seq, D_MODEL)


_LAYER_KEYS = ("ffn_norm", "ffn_w_gate", "ffn_w_up", "ffn_w_down", "mix_norm", "w_in",
               "gdn_conv_w", "gdn_a_log", "gdn_dt_bias", "gdn_norm", "gla_w_gup", "gla_b_g",
               "gla_norm", "ssd_conv_w", "ssd_conv_b", "ssd_a_log", "ssd_dt_bias", "ssd_d",
               "ssd_norm", "w_branch", "w_out")


def _trunk(x, stacked, final_norm):
    depth = stacked["w_in"].shape[0]
    for i in range(depth):
        p = {k: v[i] for k, v in stacked.items()}
        x = _layer(x, p, final_norm if i == depth - 1 else None)
    return x


def kernel(x_prompt, x_sample, ffn_norm, ffn_w_gate, ffn_w_up, ffn_w_down, mix_norm, w_in, gdn_conv_w, gdn_a_log, gdn_dt_bias, gdn_norm, gla_w_gup, gla_b_g, gla_norm, ssd_conv_w, ssd_conv_b, ssd_a_log, ssd_dt_bias, ssd_d, ssd_norm, w_branch, w_out, final_norm):
    stacked = dict(zip(_LAYER_KEYS, (
        ffn_norm, ffn_w_gate, ffn_w_up, ffn_w_down, mix_norm, w_in, gdn_conv_w, gdn_a_log,
        gdn_dt_bias, gdn_norm, gla_w_gup, gla_b_g, gla_norm, ssd_conv_w, ssd_conv_b, ssd_a_log,
        ssd_dt_bias, ssd_d, ssd_norm, w_branch, w_out)))
    assert x_prompt.shape[1:] == x_sample.shape[1:]
    nb = x_prompt.shape[0]
    y = _trunk(jnp.concatenate([x_prompt, x_sample], axis=0), stacked, final_norm)
    return y[:nb], y[nb:]
```
